```python
import math
import jax, jax.numpy as jnp
from jax import lax
import numpy as np

D_MODEL = 1024
BATCH = 8
SEQ = 8192
DEPTH = 2

HEAD_DIM = 64
Q_BLOCK = 128
RMS_EPS = 1e-6
FOX_HEADS = 8
DSA_HEADS = 8
IDX_HEADS = 4
IDX_DIM = 64
DSA_TOPK = 256
DIFF_HEADS = 4
DIFF_VDIM = 2 * HEAD_DIM
MLA_HEADS = 8
MLA_NOPE = 64
MLA_ROPE = 32
MLA_VDIM = 64
MLA_Q_RANK = 256
MLA_KV_RANK = 128
ROPE_THETA = 10000.0
T5_BUCKETS = 32
T5_MAX_DIST = 128
T5_HEADS = 8
FFN_HIDDEN = 256 * math.ceil(8 * D_MODEL / (3 * 256))

FOX_W = FOX_HEADS * HEAD_DIM
DSA_W = DSA_HEADS * HEAD_DIM
EVEN_SPLITS = [FOX_W, FOX_W, FOX_W, FOX_HEADS, DSA_W, DSA_W, DSA_W,
               IDX_HEADS * IDX_DIM, IDX_DIM, IDX_HEADS]
EVEN_IN = sum(EVEN_SPLITS)
EVEN_MIX = FOX_W + DSA_W
DIFF_QK_W = DIFF_HEADS * 2 * HEAD_DIM
DIFF_V_W = DIFF_HEADS * DIFF_VDIM
MLA_OUT_W = MLA_HEADS * MLA_VDIM
ODD_SPLITS = [DIFF_QK_W, DIFF_QK_W, DIFF_V_W, MLA_Q_RANK, MLA_KV_RANK, MLA_ROPE]
ODD_IN = sum(ODD_SPLITS)
ODD_MIX = DIFF_V_W + MLA_OUT_W
N_EVEN = (DEPTH + 1) // 2
N_ODD = DEPTH // 2

kernel_name = "hybrid_fox_dsa_diff_mla_block"

F32 = jnp.float32


def rms_norm(x, g):
    x32 = x.astype(F32)
    y = x32 * lax.rsqrt(jnp.mean(x32 * x32, axis=-1, keepdims=True) + RMS_EPS)
    return (y * g.astype(F32)).astype(x.dtype)


def split_cols(y, sizes):
    offs = [int(o) for o in np.cumsum(sizes)[:-1]]
    return jnp.split(y, offs, axis=-1)


def causal_mask(t_pos, s_len):
    return jnp.arange(s_len)[None, :] <= t_pos[:, None]


def masked_softmax(logits, mask):
    return jax.nn.softmax(jnp.where(mask, logits, -jnp.inf), axis=-1)


def t5_bucket(dist):
    exact = T5_BUCKETS // 2
    d = jnp.maximum(dist, 1).astype(F32)
    log_b = exact + (jnp.log(d / exact) / math.log(T5_MAX_DIST / exact)
                     * (T5_BUCKETS - exact)).astype(jnp.int32)
    log_b = jnp.minimum(log_b, T5_BUCKETS - 1)
    return jnp.where(dist < exact, dist, log_b)


def dense_t5_bias(t_pos, s_len, table):
    dist = jnp.maximum(t_pos[:, None] - jnp.arange(s_len)[None, :], 0)
    return jnp.transpose(table[t5_bucket(dist)], (2, 0, 1)).astype(F32)


def sweep_query_blocks(block_fn, seq_len):
    n = seq_len // Q_BLOCK
    out = lax.map(block_fn, jnp.arange(n))
    out = jnp.moveaxis(out, 0, 1)
    return out.reshape(out.shape[0], n * Q_BLOCK, *out.shape[3:])


def rope_tables(s_len):
    pos = jnp.arange(s_len, dtype=F32)
    inv = ROPE_THETA ** (-jnp.arange(0, MLA_ROPE, 2, dtype=F32) / MLA_ROPE)
    ang = pos[:, None] * inv[None, :]
    ang = jnp.concatenate([ang, ang], axis=-1)
    return jnp.cos(ang), jnp.sin(ang)


def apply_rope(x, cos, sin):
    x32 = x.astype(F32)
    x1, x2 = jnp.split(x32, 2, axis=-1)
    rot = jnp.concatenate([-x2, x1], axis=-1)
    return (x32 * cos + rot * sin).astype(x.dtype)


def fox_attention(q, k, v, log_cum):
    s_len = q.shape[1]
    scale = HEAD_DIM ** -0.5

    def block(i):
        start = i * Q_BLOCK
        t_pos = start + jnp.arange(Q_BLOCK)
        qb = lax.dynamic_slice_in_dim(q, start, Q_BLOCK, 1)
        cb = lax.dynamic_slice_in_dim(log_cum, start, Q_BLOCK, 2)
        logits = jnp.einsum('bqhd,bkhd->bhqk', qb, k, preferred_element_type=F32) * scale
        logits = logits + (cb[..., None] - log_cum[:, :, None, :])
        p = masked_softmax(logits, causal_mask(t_pos, s_len))
        return jnp.einsum('bhqk,bkhd->bqhd', p.astype(v.dtype), v)

    return sweep_query_blocks(block, s_len)


def dsa_attention(q, k, v, q_idx, k_idx, w_idx, t5_table):
    s_len = q.shape[1]
    k_sel = min(DSA_TOPK, s_len // 4)
    scale = HEAD_DIM ** -0.5
    idx_scale = (IDX_DIM ** -0.5) * (IDX_HEADS ** -0.5)
    gather = jax.vmap(lambda a, ix: a[ix])

    def block(i):
        start = i * Q_BLOCK
        t_pos = start + jnp.arange(Q_BLOCK)
        qb = lax.dynamic_slice_in_dim(q, start, Q_BLOCK, 1)
        qib = lax.dynamic_slice_in_dim(q_idx, start, Q_BLOCK, 1)
        wb = lax.dynamic_slice_in_dim(w_idx, start, Q_BLOCK, 1)
        rel = jax.nn.relu(jnp.einsum('bqhd,bkd->bqhk', qib, k_idx, preferred_element_type=F32))
        score = jnp.einsum('bqhk,bqh->bqk', rel, wb.astype(F32)) * idx_scale
        score = jnp.where(causal_mask(t_pos, s_len)[None], score, -jnp.inf)
        _, sel = lax.top_k(score, k_sel)
        valid = sel <= t_pos[None, :, None]
        ks = gather(k, sel)
        vs = gather(v, sel)
        logits = jnp.einsum('bqhd,bqkhd->bhqk', qb, ks, preferred_element_type=F32) * scale
        dist = jnp.maximum(t_pos[None, :, None] - sel, 0)
        bias = t5_table[t5_bucket(dist)]
        logits = logits + jnp.transpose(bias, (0, 3, 1, 2)).astype(F32)
        p = masked_softmax(logits, valid[:, None])
        return jnp.einsum('bhqk,bqkhd->bqhd', p.astype(vs.dtype), vs)

    return sweep_query_blocks(block, s_len)


def diff_attention(q, k, v, lam, t5_table):
    bsz, s_len = q.shape[:2]
    scale = HEAD_DIM ** -0.5

    def block(i):
        start = i * Q_BLOCK
        t_pos = start + jnp.arange(Q_BLOCK)
        qb = lax.dynamic_slice_in_dim(q, start, Q_BLOCK, 1)
        logits = jnp.einsum('bqhd,bkhd->bhqk', qb, k, preferred_element_type=F32) * scale
        logits = logits + dense_t5_bias(t_pos, s_len, t5_table)[None]
        p = masked_softmax(logits, causal_mask(t_pos, s_len))
        p = p.reshape(bsz, DIFF_HEADS, 2, Q_BLOCK, s_len)
        a = p[:, :, 0] - lam * p[:, :, 1]
        return jnp.einsum('bhqk,bkhd->bqhd', a.astype(v.dtype), v)

    return sweep_query_blocks(block, s_len)


def mla_attention(q_nope, q_rope, k_nope, k_rope, v):
    s_len = q_nope.shape[1]
    scale = (MLA_NOPE + MLA_ROPE) ** -0.5

    def block(i):
        start = i * Q_BLOCK
        t_pos = start + jnp.arange(Q_BLOCK)
        qn = lax.dynamic_slice_in_dim(q_nope, start, Q_BLOCK, 1)
        qr = lax.dynamic_slice_in_dim(q_rope, start, Q_BLOCK, 1)
        logits = (jnp.einsum('bqhd,bkhd->bhqk', qn, k_nope, preferred_element_type=F32)
                  + jnp.einsum('bqhr,bkr->bhqk', qr, k_rope, preferred_element_type=F32)) * scale
        p = masked_softmax(logits, causal_mask(t_pos, s_len))
        return jnp.einsum('bhqk,bkhd->bqhd', p.astype(v.dtype), v)

    return sweep_query_blocks(block, s_len)


def even_mixer(h, w_in, b_forget, w_out, t5_table):
    bsz, s_len, _ = h.shape
    y = h @ w_in
    fq, fk, fv, ff, dq, dk, dv, iq, ik, iw = split_cols(y, EVEN_SPLITS)
    hd = lambda a, n, d: a.reshape(bsz, s_len, n, d)
    log_f = jax.nn.log_sigmoid(ff.astype(F32) + b_forget.astype(F32))
    log_cum = jnp.transpose(jnp.cumsum(log_f, axis=1), (0, 2, 1))
    fox_out = fox_attention(hd(fq, FOX_HEADS, HEAD_DIM), hd(fk, FOX_HEADS, HEAD_DIM),
                            hd(fv, FOX_HEADS, HEAD_DIM), log_cum)
    dsa_out = dsa_attention(hd(dq, DSA_HEADS, HEAD_DIM), hd(dk, DSA_HEADS, HEAD_DIM),
                            hd(dv, DSA_HEADS, HEAD_DIM), hd(iq, IDX_HEADS, IDX_DIM),
                            ik, iw, t5_table)
    mixed = jnp.concatenate([fox_out.reshape(bsz, s_len, FOX_W),
                             dsa_out.reshape(bsz, s_len, DSA_W)], axis=-1)
    return mixed @ w_out


def odd_mixer(h, w_in, lq1, lk1, lq2, lk2, subln_g, q_norm_g, w_uq, kv_norm_g, w_ukv,
              w_out, t5_table, lambda_init):
    bsz, s_len, _ = h.shape
    y = h @ w_in
    cq, ck, cv, mcq, mckv, mkr = split_cols(y, ODD_SPLITS)
    lam = (jnp.exp(jnp.sum(lq1.astype(F32) * lk1.astype(F32)))
           - jnp.exp(jnp.sum(lq2.astype(F32) * lk2.astype(F32))) + lambda_init)
    diff_out = diff_attention(cq.reshape(bsz, s_len, 2 * DIFF_HEADS, HEAD_DIM),
                              ck.reshape(bsz, s_len, 2 * DIFF_HEADS, HEAD_DIM),
                              cv.reshape(bsz, s_len, DIFF_HEADS, DIFF_VDIM), lam, t5_table)
    diff_out = rms_norm(diff_out, subln_g) * (1.0 - lambda_init)
    q = (rms_norm(mcq, q_norm_g) @ w_uq).reshape(bsz, s_len, MLA_HEADS, MLA_NOPE + MLA_ROPE)
    q_nope, q_rope = q[..., :MLA_NOPE], q[..., MLA_NOPE:]
    kv = (rms_norm(mckv, kv_norm_g) @ w_ukv).reshape(bsz, s_len, MLA_HEADS, MLA_NOPE + MLA_VDIM)
    k_nope, v = kv[..., :MLA_NOPE], kv[..., MLA_NOPE:]
    cos, sin = rope_tables(s_len)
    q_rope = apply_rope(q_rope, cos[:, None, :], sin[:, None, :])
    k_rope = apply_rope(mkr, cos, sin)
    mla_out = mla_attention(q_nope, q_rope, k_nope, k_rope, v)
    mixed = jnp.concatenate([diff_out.reshape(bsz, s_len, DIFF_V_W),
                             mla_out.reshape(bsz, s_len, MLA_OUT_W)], axis=-1)
    return mixed @ w_out


def swiglu(h, w_gate, w_up, w_down):
    return (jax.nn.silu(h @ w_gate) * (h @ w_up)) @ w_down


def setup_inputs(seed: int = 0) -> dict:
    key = jax.random.key(seed)
    ks = jax.random.split(key, 24)
    nrm = lambda k, shape, fan_in: jax.random.normal(k, shape, F32) * (fan_in ** -0.5)
    gain = lambda k, shape: 1.0 + 0.02 * jax.random.normal(k, shape, F32)
    return {
        "x": jax.random.normal(ks[0], (BATCH, SEQ, D_MODEL), F32),
        "norm_mix_g": gain(ks[1], (DEPTH, D_MODEL)),
        "norm_ffn_g": gain(ks[2], (DEPTH, D_MODEL)),
        "w_in_even": nrm(ks[3], (N_EVEN, D_MODEL, EVEN_IN), D_MODEL),
        "b_forget": jax.random.uniform(ks[4], (N_EVEN, FOX_HEADS), F32, 1.0, 4.0),
        "w_out_even": nrm(ks[5], (N_EVEN, EVEN_MIX, D_MODEL), EVEN_MIX),
        "w_in_odd": nrm(ks[6], (N_ODD, D_MODEL, ODD_IN), D_MODEL),
        "lambda_q1": 0.1 * jax.random.normal(ks[7], (N_ODD, HEAD_DIM), F32),
        "lambda_k1": 0.1 * jax.random.normal(ks[8], (N_ODD, HEAD_DIM), F32),
        "lambda_q2": 0.1 * jax.random.normal(ks[9], (N_ODD, HEAD_DIM), F32),
        "lambda_k2": 0.1 * jax.random.normal(ks[10], (N_ODD, HEAD_DIM), F32),
        "diff_subln_g": gain(ks[11], (N_ODD, DIFF_VDIM)),
        "mla_q_norm_g": gain(ks[12], (N_ODD, MLA_Q_RANK)),
        "w_mla_uq": nrm(ks[13], (N_ODD, MLA_Q_RANK, MLA_HEADS * (MLA_NOPE + MLA_ROPE)), MLA_Q_RANK),
        "mla_kv_norm_g": gain(ks[14], (N_ODD, MLA_KV_RANK)),
        "w_mla_ukv": nrm(ks[15], (N_ODD, MLA_KV_RANK, MLA_HEADS * (MLA_NOPE + MLA_VDIM)), MLA_KV_RANK),
        "w_out_odd": nrm(ks[16], (N_ODD, ODD_MIX, D_MODEL), ODD_MIX),
        "t5_bias": 0.5 * jax.random.normal(ks[17], (T5_BUCKETS, T5_HEADS), F32),
        "w_ffn_gate": nrm(ks[18], (DEPTH, D_MODEL, FFN_HIDDEN), D_MODEL),
        "w_ffn_up": nrm(ks[19], (DEPTH, D_MODEL, FFN_HIDDEN), D_MODEL),
        "w_ffn_down": nrm(ks[20], (DEPTH, FFN_HIDDEN, D_MODEL), FFN_HIDDEN),
        "final_norm_g": gain(ks[21], (D_MODEL,)),
    }


def reference(x, norm_mix_g, norm_ffn_g, w_in_even, b_forget, w_out_even, w_in_odd,
              lambda_q1, lambda_k1, lambda_q2, lambda_k2, diff_subln_g, mla_q_norm_g,
              w_mla_uq, mla_kv_norm_g, w_mla_ukv, w_out_odd, t5_bias,
              w_ffn_gate, w_ffn_up, w_ffn_down, final_norm_g):
    for layer in range(DEPTH):
        j = layer // 2
        h = rms_norm(x, norm_mix_g[layer])
        if layer % 2 == 0:
            x = x + even_mixer(h, w_in_even[j], b_forget[j], w_out_even[j], t5_bias)
        else:
            lambda_init = 0.8 - 0.6 * math.exp(-0.3 * layer)
            x = x + odd_mixer(h, w_in_odd[j], lambda_q1[j], lambda_k1[j], lambda_q2[j],
                              lambda_k2[j], diff_subln_g[j], mla_q_norm_g[j], w_mla_uq[j],
                              mla_kv_norm_g[j], w_mla_ukv[j], w_out_odd[j], t5_bias,
                              lambda_init)
        h = rms_norm(x, norm_ffn_g[layer])
        x = x + swiglu(h, w_ffn_gate[layer], w_ffn_up[layer], w_ffn_down[layer])
    return rms_norm(x, final_norm_g)
```

```python
import functools
import math

import numpy as np
import jax
import jax.numpy as jnp
from jax import lax
from jax.experimental import pallas as pl
from jax.experimental.pallas import tpu as pltpu

F32 = jnp.float32
BF16 = jnp.bfloat16

HEAD_DIM = 64
RMS_EPS = 1e-6
FOX_HEADS = 8
DSA_HEADS = 8
IDX_HEADS = 4
IDX_DIM = 64
DSA_TOPK = 256
DIFF_HEADS = 4
DIFF_VDIM = 2 * HEAD_DIM
MLA_HEADS = 8
MLA_NOPE = 64
MLA_ROPE = 32
MLA_VDIM = 64
MLA_Q_RANK = 256
MLA_KV_RANK = 128
ROPE_THETA = 10000.0
T5_BUCKETS = 32
T5_MAX_DIST = 128

FOX_W = FOX_HEADS * HEAD_DIM
DSA_W = DSA_HEADS * HEAD_DIM
EVEN_SPLITS = [FOX_W, FOX_W, FOX_W, FOX_HEADS, DSA_W, DSA_W, DSA_W,
               IDX_HEADS * IDX_DIM, IDX_DIM, IDX_HEADS]
DIFF_QK_W = DIFF_HEADS * 2 * HEAD_DIM
DIFF_V_W = DIFF_HEADS * DIFF_VDIM
ODD_SPLITS = [DIFF_QK_W, DIFF_QK_W, DIFF_V_W, MLA_Q_RANK, MLA_KV_RANK, MLA_ROPE]

LANES = 128
NEG = -1e30
INT_MIN = -(2 ** 31)
VMEM_LIMIT = 48 * 1024 * 1024


def _cparams(sem):
    return pltpu.CompilerParams(dimension_semantics=sem, vmem_limit_bytes=VMEM_LIMIT)


def _pick(n, prefs):
    for p in prefs:
        if n % p == 0:
            return p
    return n


def _mm_kernel(*refs, has_norm, has_res):
    it = iter(refs)
    x_ref = next(it)
    g_ref = next(it) if has_norm else None
    w_ref = next(it)
    r_ref = next(it) if has_res else None
    o_ref = next(it)
    xn_ref = next(it)

    @pl.when(pl.program_id(1) == 0)
    def _():
        x = x_ref[...].astype(F32)
        if has_norm:
            ms = jnp.mean(x * x, axis=-1, keepdims=True)
            x = x * lax.rsqrt(ms + RMS_EPS) * g_ref[...]
        xn_ref[...] = x.astype(BF16)

    acc = jnp.dot(xn_ref[...], w_ref[...], preferred_element_type=F32)
    if has_res:
        acc = acc + r_ref[...]
    o_ref[...] = acc.astype(o_ref.dtype)


def _mm(x, w, g=None, res=None, out_dtype=F32):
    m, k = x.shape
    n = w.shape[1]
    tm = _pick(m, (512, 256, 128))
    tn = _pick(n, (1152, 1024, 768, 512, 384, 256, 128))
    args = [x]
    specs = [pl.BlockSpec((tm, k), lambda i, j: (i, 0))]
    if g is not None:
        args.append(g.reshape(1, k).astype(F32))
        specs.append(pl.BlockSpec((1, k), lambda i, j: (0, 0)))
    args.append(w)
    specs.append(pl.BlockSpec((k, tn), lambda i, j: (0, j)))
    if res is not None:
        args.append(res)
        specs.append(pl.BlockSpec((tm, tn), lambda i, j: (i, j)))
    return pl.pallas_call(
        functools.partial(_mm_kernel, has_norm=g is not None, has_res=res is not None),
        grid=(m // tm, n // tn),
        in_specs=specs,
        out_specs=pl.BlockSpec((tm, tn), lambda i, j: (i, j)),
        out_shape=jax.ShapeDtypeStruct((m, n), out_dtype),
        scratch_shapes=[pltpu.VMEM((tm, k), BF16)],
        compiler_params=_cparams(("parallel", "arbitrary")),
        name="mm",
    )(*args)


def _ffn_kernel(*refs, has_final):
    it = iter(refs)
    x_ref, g_ref, wg_ref, wu_ref, wd_ref = (next(it) for _ in range(5))
    fg_ref = next(it) if has_final else None
    o_ref, hn_ref, acc_ref = next(it), next(it), next(it)
    f = pl.program_id(1)

    @pl.when(f == 0)
    def _():
        x = x_ref[...]
        ms = jnp.mean(x * x, axis=-1, keepdims=True)
        hn_ref[...] = (x * lax.rsqrt(ms + RMS_EPS) * g_ref[...]).astype(BF16)
        acc_ref[...] = jnp.zeros_like(acc_ref)

    h = hn_ref[...]
    gate = jnp.dot(h, wg_ref[...], preferred_element_type=F32)
    up = jnp.dot(h, wu_ref[...], preferred_element_type=F32)
    act = gate * jax.nn.sigmoid(gate) * up
    acc_ref[...] += jnp.dot(act.astype(BF16), wd_ref[...], preferred_element_type=F32)

    @pl.when(f == pl.num_programs(1) - 1)
    def _():
        y = x_ref[...] + acc_ref[...]
        if has_final:
            ms = jnp.mean(y * y, axis=-1, keepdims=True)
            y = y * lax.rsqrt(ms + RMS_EPS) * fg_ref[...]
        o_ref[...] = y


def _ffn(x, g, wg, wu, wd, final_g=None):
    m, d = x.shape
    hid = wg.shape[1]
    tm = _pick(m, (512, 256, 128))
    tf = _pick(hid, (256, 128))
    args = [x, g.reshape(1, d).astype(F32), wg, wu, wd]
    specs = [pl.BlockSpec((tm, d), lambda i, f: (i, 0)),
             pl.BlockSpec((1, d), lambda i, f: (0, 0)),
             pl.BlockSpec((d, tf), lambda i, f: (0, f)),
             pl.BlockSpec((d, tf), lambda i, f: (0, f)),
             pl.BlockSpec((tf, d), lambda i, f: (f, 0))]
    if final_g is not None:
        args.append(final_g.reshape(1, d).astype(F32))
        specs.append(pl.BlockSpec((1, d), lambda i, f: (0, 0)))
    return pl.pallas_call(
        functools.partial(_ffn_kernel, has_final=final_g is not None),
        grid=(m // tm, hid // tf),
        in_specs=specs,
        out_specs=pl.BlockSpec((tm, d), lambda i, f: (i, 0)),
        out_shape=jax.ShapeDtypeStruct((m, d), F32),
        scratch_shapes=[pltpu.VMEM((tm, d), BF16), pltpu.VMEM((tm, d), F32)],
        compiler_params=_cparams(("parallel", "arbitrary")),
        name="ffn",
    )(*args)


def _dsa_mask_kernel(qi_ref, ki_ref, w_ref, o_ref, keys_ref, *, tile, nk, ksel, idx_scale):
    i = pl.program_id(1)
    groups = tile // LANES
    w = w_ref[0]
    row = lax.broadcasted_iota(jnp.int32, (tile, tile), 0)
    col = lax.broadcasted_iota(jnp.int32, (tile, tile), 1)
    kf = float(ksel)

    def score_chunk(j, diag):
        kc = ki_ref[0, j]
        sc = jnp.zeros((tile, tile), F32)
        for h in range(IDX_HEADS):
            d = jnp.dot(qi_ref[0, h], kc, preferred_element_type=F32)
            sc = sc + jnp.maximum(d, 0.0) * w[:, h:h + 1]
        sc = sc * idx_scale
        bits = lax.bitcast_convert_type(sc, jnp.int32)
        key = jnp.where(bits < 0, INT_MIN - bits, bits)
        if diag:
            key = jnp.where(col <= row, key, INT_MIN)
        keys_ref[j] = key

    def _plain(j, c):
        score_chunk(j, False)
        return c

    lax.fori_loop(0, i, _plain, 0)
    score_chunk(i, True)

    ones_l = jnp.ones((LANES, LANES), BF16)

    def count(pred):
        def body(j, acc):
            kc = keys_ref[j]
            for c in range(groups):
                acc = acc + jnp.where(pred(kc[:, c * LANES:(c + 1) * LANES]), 1.0, 0.0)
            return acc
        acc = lax.fori_loop(0, i + 1, body, jnp.zeros((tile, LANES), F32))
        return jnp.dot(acc.astype(BF16), ones_l, preferred_element_type=F32)

    c0 = count(lambda k: k >= 0)
    prefix = jnp.where(c0 >= kf, 0, INT_MIN).astype(jnp.int32)

    def bit_body(b, prefix):
        cand = prefix + jnp.left_shift(jnp.int32(1), 30 - b)
        c = count(lambda k: k >= cand)
        return jnp.where(c >= kf, cand, prefix)

    tau = lax.fori_loop(0, 31, bit_body, prefix)
    need = kf - count(lambda k: k > tau)

    tri = (row <= col).astype(BF16)
    ones_t = jnp.ones((tile, LANES), BF16)
    tau_t = jnp.concatenate([tau] * groups, axis=1)
    need_t = jnp.concatenate([need] * groups, axis=1)

    def emit(j, carry):
        kc = keys_ref[j]
        eq = kc == tau_t
        eqf = jnp.where(eq, 1.0, 0.0).astype(BF16)
        rank = jnp.dot(eqf, tri, preferred_element_type=F32) + jnp.concatenate([carry] * groups, axis=1)
        sel = (kc > tau_t) | (eq & (rank <= need_t) & (kc > INT_MIN))
        o_ref[0, j] = jnp.where(sel, 0.0, NEG).astype(o_ref.dtype)
        return carry + jnp.dot(eqf, ones_t, preferred_element_type=F32)

    lax.fori_loop(0, i + 1, emit, jnp.zeros((tile, LANES), F32))

    def _fill(j, c):
        o_ref[0, j] = jnp.full((tile, tile), NEG, o_ref.dtype)
        return c

    lax.fori_loop(i + 1, nk, _fill, 0)


def _dsa_mask(qi3, ki3, w, tile, ksel):
    bsz, hi, s_len, d3 = qi3.shape
    nk = s_len // tile
    assert s_len // LANES <= 256, "per-lane counts must stay exact in bf16"
    idx_scale = (IDX_DIM ** -0.5) * (IDX_HEADS ** -0.5)
    return pl.pallas_call(
        functools.partial(_dsa_mask_kernel, tile=tile, nk=nk, ksel=ksel, idx_scale=idx_scale),
        grid=(bsz, nk),
        in_specs=[pl.BlockSpec((1, hi, tile, d3), lambda b, i: (b, 0, i, 0)),
                  pl.BlockSpec((1, nk, d3, tile), lambda b, i: (b, 0, 0, 0)),
                  pl.BlockSpec((1, tile, hi), lambda b, i: (b, i, 0))],
        out_specs=pl.BlockSpec((1, nk, tile, tile), lambda b, i: (b, 0, i, 0)),
        out_shape=jax.ShapeDtypeStruct((bsz, nk, s_len, tile), BF16),
        scratch_shapes=[pltpu.VMEM((nk, tile, tile), jnp.int32)],
        compiler_params=_cparams(("parallel", "arbitrary")),
        name="dsa_mask",
    )(qi3, ki3, w)


def _flash_kernel(*refs, tile, has_kbias, has_tiles, has_mask):
    it = iter(refs)
    q_ref, k_ref, v_ref = next(it), next(it), next(it)
    kb_ref = next(it) if has_kbias else None
    t_ref = next(it) if has_tiles else None
    mk_ref = next(it) if has_mask else None
    o_ref, m_ref, l_ref, acc_ref = next(it), next(it), next(it), next(it)

    i = pl.program_id(2)
    q = q_ref[0, 0]
    m_ref[...] = jnp.full_like(m_ref, NEG)
    l_ref[...] = jnp.zeros_like(l_ref)
    acc_ref[...] = jnp.zeros_like(acc_ref)

    def step(j, rel_tile=None, diag=False):
        s = jnp.dot(q, k_ref[0, 0, j], preferred_element_type=F32)
        if has_kbias:
            s = s + kb_ref[0, 0, j]
        if rel_tile is not None:
            s = s + t_ref[0, rel_tile]
        if has_mask:
            s = s + mk_ref[0, j].astype(F32)
        elif diag:
            row = lax.broadcasted_iota(jnp.int32, (tile, tile), 0)
            col = lax.broadcasted_iota(jnp.int32, (tile, tile), 1)
            s = jnp.where(col <= row, s, NEG)
        m_old = m_ref[...]
        m_new = jnp.maximum(m_old, jnp.max(s, axis=-1, keepdims=True))
        p = jnp.exp(s - m_new)
        alpha = jnp.exp(m_old - m_new)
        l_ref[...] = alpha * l_ref[...] + jnp.sum(p, axis=-1, keepdims=True)
        acc_ref[...] = alpha * acc_ref[...] + jnp.dot(
            p.astype(BF16), v_ref[0, 0, j], preferred_element_type=F32)
        m_ref[...] = m_new

    def _far(j, c):
        step(j)
        return c

    if has_tiles:
        lax.fori_loop(0, i - 1, _far, 0)

        @pl.when(i >= 1)
        def _():
            step(i - 1, rel_tile=1)

        step(i, rel_tile=0, diag=True)
    else:
        lax.fori_loop(0, i, _far, 0)
        step(i, diag=True)

    o_ref[0, 0] = (acc_ref[...] / l_ref[...]).astype(o_ref.dtype)


def _flash(q, kt, v, tile, kbias=None, rel_tiles=None, mask=None, v_group=1):
    bsz, nh, s_len, dq = q.shape
    nk = s_len // tile
    dv = v.shape[-1]
    args = [q, kt, v]
    specs = [pl.BlockSpec((1, 1, tile, dq), lambda b, h, i: (b, h, i, 0)),
             pl.BlockSpec((1, 1, nk, dq, tile), lambda b, h, i: (b, h, 0, 0, 0)),
             pl.BlockSpec((1, 1, nk, tile, dv), lambda b, h, i: (b, h // v_group, 0, 0, 0))]
    if kbias is not None:
        args.append(kbias)
        specs.append(pl.BlockSpec((1, 1, nk, 1, tile), lambda b, h, i: (b, h, 0, 0, 0)))
    if rel_tiles is not None:
        args.append(rel_tiles)
        specs.append(pl.BlockSpec((1, 2, tile, tile), lambda b, h, i: (h, 0, 0, 0)))
    if mask is not None:
        args.append(mask)
        specs.append(pl.BlockSpec((1, nk, tile, tile), lambda b, h, i: (b, 0, i, 0)))
    return pl.pallas_call(
        functools.partial(_flash_kernel, tile=tile, has_kbias=kbias is not None,
                          has_tiles=rel_tiles is not None, has_mask=mask is not None),
        grid=(bsz, nh, nk),
        in_specs=specs,
        out_specs=pl.BlockSpec((1, 1, tile, dv), lambda b, h, i: (b, h, i, 0)),
        out_shape=jax.ShapeDtypeStruct((bsz, nh, s_len, dv), F32),
        scratch_shapes=[pltpu.VMEM((tile, 1), F32), pltpu.VMEM((tile, 1), F32),
                        pltpu.VMEM((tile, dv), F32)],
        compiler_params=_cparams(("parallel", "parallel", "arbitrary")),
        name="flash",
    )(*args)


def _t5_bucket_map(tile):
    exact = T5_BUCKETS // 2
    r = np.arange(tile)[:, None]
    c = np.arange(tile)[None, :]
    out = []
    for off in (0, tile):
        dist = np.maximum(off + r - c, 0)
        d = np.maximum(dist, 1).astype(np.float32)
        log_b = exact + (np.log(d / np.float32(exact)) / np.float32(math.log(T5_MAX_DIST / exact))
                         * np.float32(T5_BUCKETS - exact)).astype(np.int32)
        log_b = np.minimum(log_b, T5_BUCKETS - 1)
        out.append(np.where(dist < exact, dist, log_b))
    return np.stack(out).astype(np.int32)


def _rel_tiles(t5_table, tile):
    assert tile >= T5_MAX_DIST
    bmap = _t5_bucket_map(tile)
    t = t5_table.astype(F32)
    tiles = t[bmap] - t[T5_BUCKETS - 1]
    return jnp.transpose(tiles, (3, 0, 1, 2))


def _split_cols(y, sizes):
    offs = np.cumsum([0] + list(sizes))
    return [y[..., int(offs[k]):int(offs[k + 1])] for k in range(len(sizes))]


def _q_heads(a, n, d, scale):
    b, s, _ = a.shape
    return jnp.transpose((a * scale).reshape(b, s, n, d), (0, 2, 1, 3)).astype(BF16)


def _kt_chunks(a, n, d, tile):
    b, s, _ = a.shape
    return jnp.transpose(a.reshape(b, s // tile, tile, n, d), (0, 3, 1, 4, 2)).astype(BF16)


def _v_chunks(a, n, d, tile):
    b, s, _ = a.shape
    return jnp.transpose(a.reshape(b, s // tile, tile, n, d), (0, 3, 1, 2, 4)).astype(BF16)


def _merge_heads(o):
    b, n, s, d = o.shape
    return jnp.transpose(o, (0, 2, 1, 3)).reshape(b, s, n * d)


def _hi_lo(a):
    hi = a.astype(BF16)
    lo = (a - hi.astype(F32)).astype(BF16)
    return hi, lo


def _pad_cols(w, mult=LANES):
    n = w.shape[-1]
    pad = (-n) % mult
    return jnp.pad(w, ((0, 0), (0, pad))) if pad else w


def _rope_tables(s_len):
    pos = jnp.arange(s_len, dtype=F32)
    inv = ROPE_THETA ** (-jnp.arange(0, MLA_ROPE, 2, dtype=F32) / MLA_ROPE)
    ang = pos[:, None] * inv[None, :]
    ang = jnp.concatenate([ang, ang], axis=-1)
    return jnp.cos(ang), jnp.sin(ang)


def _apply_rope(x, cos, sin):
    x1, x2 = jnp.split(x, 2, axis=-1)
    rot = jnp.concatenate([-x2, x1], axis=-1)
    return x * cos + rot * sin


def _even_mixer(x, g, w_in, b_forget, w_out, t5_table, tile):
    bsz, s_len, dm = x.shape
    m = bsz * s_len
    x2 = x.reshape(m, dm)
    y = _mm(x2, _pad_cols(w_in).astype(BF16), g=g).reshape(bsz, s_len, -1)
    fq, fk, fv, ff, dq, dk, dv, iq, ik, iw = _split_cols(y[..., :sum(EVEN_SPLITS)], EVEN_SPLITS)
    nk = s_len // tile
    scale = HEAD_DIM ** -0.5

    log_f = jax.nn.log_sigmoid(ff + b_forget.astype(F32))
    log_cum = jnp.cumsum(log_f, axis=1)
    kbias = (-jnp.transpose(log_cum, (0, 2, 1))).reshape(bsz, FOX_HEADS, nk, 1, tile)
    fox = _flash(_q_heads(fq, FOX_HEADS, HEAD_DIM, scale), _kt_chunks(fk, FOX_HEADS, HEAD_DIM, tile),
                 _v_chunks(fv, FOX_HEADS, HEAD_DIM, tile), tile, kbias=kbias)

    q_hi, q_lo = _hi_lo(iq.reshape(bsz, s_len, IDX_HEADS, IDX_DIM))
    k_hi, k_lo = _hi_lo(ik)
    qi3 = jnp.transpose(jnp.concatenate([q_hi, q_hi, q_lo], axis=-1), (0, 2, 1, 3))
    ki3 = jnp.transpose(jnp.concatenate([k_hi, k_lo, k_hi], axis=-1).reshape(bsz, nk, tile, 3 * IDX_DIM),
                        (0, 1, 3, 2))
    mask = _dsa_mask(qi3, ki3, iw, tile, min(DSA_TOPK, s_len // 4))
    dsa = _flash(_q_heads(dq, DSA_HEADS, HEAD_DIM, scale), _kt_chunks(dk, DSA_HEADS, HEAD_DIM, tile),
                 _v_chunks(dv, DSA_HEADS, HEAD_DIM, tile), tile,
                 rel_tiles=_rel_tiles(t5_table, tile), mask=mask)

    mixed = jnp.concatenate([_merge_heads(fox), _merge_heads(dsa)], axis=-1).reshape(m, -1)
    return _mm(mixed, w_out.astype(BF16), res=x2).reshape(bsz, s_len, dm)


def _odd_mixer(x, g, w_in, lq1, lk1, lq2, lk2, subln_g, q_norm_g, w_uq, kv_norm_g, w_ukv,
               w_out, t5_table, lambda_init, tile):
    bsz, s_len, dm = x.shape
    m = bsz * s_len
    x2 = x.reshape(m, dm)
    y = _mm(x2, _pad_cols(w_in).astype(BF16), g=g)
    cq, ck, cv, mcq, mckv, mkr = _split_cols(y[:, :sum(ODD_SPLITS)], ODD_SPLITS)
    r3 = lambda a: a.reshape(bsz, s_len, -1)

    lam = (jnp.exp(jnp.sum(lq1.astype(F32) * lk1.astype(F32)))
           - jnp.exp(jnp.sum(lq2.astype(F32) * lk2.astype(F32))) + lambda_init)
    maps = _flash(_q_heads(r3(cq), 2 * DIFF_HEADS, HEAD_DIM, HEAD_DIM ** -0.5),
                  _kt_chunks(r3(ck), 2 * DIFF_HEADS, HEAD_DIM, tile),
                  _v_chunks(r3(cv), DIFF_HEADS, DIFF_VDIM, tile), tile,
                  rel_tiles=_rel_tiles(t5_table, tile), v_group=2)
    maps = maps.reshape(bsz, DIFF_HEADS, 2, s_len, DIFF_VDIM)
    diff = maps[:, :, 0] - lam * maps[:, :, 1]
    ms = jnp.mean(diff * diff, axis=-1, keepdims=True)
    diff = diff * lax.rsqrt(ms + RMS_EPS) * subln_g.astype(F32) * (1.0 - lambda_init)

    q = _mm(mcq, w_uq.astype(BF16), g=q_norm_g).reshape(bsz, s_len, MLA_HEADS, MLA_NOPE + MLA_ROPE)
    kv = _mm(mckv, w_ukv.astype(BF16), g=kv_norm_g).reshape(bsz, s_len, MLA_HEADS, MLA_NOPE + MLA_VDIM)
    cos, sin = _rope_tables(s_len)
    q_rope = _apply_rope(q[..., MLA_NOPE:], cos[:, None, :], sin[:, None, :])
    k_rope = _apply_rope(r3(mkr), cos, sin)
    dqk = MLA_NOPE + MLA_ROPE
    q_cat = jnp.concatenate([q[..., :MLA_NOPE], q_rope], axis=-1).reshape(bsz, s_len, MLA_HEADS * dqk)
    k_cat = jnp.concatenate(
        [kv[..., :MLA_NOPE], jnp.broadcast_to(k_rope[:, :, None, :], (bsz, s_len, MLA_HEADS, MLA_ROPE))],
        axis=-1).reshape(bsz, s_len, MLA_HEADS * dqk)
    v = kv[..., MLA_NOPE:].reshape(bsz, s_len, MLA_HEADS * MLA_VDIM)
    mla = _flash(_q_heads(q_cat, MLA_HEADS, dqk, dqk ** -0.5), _kt_chunks(k_cat, MLA_HEADS, dqk, tile),
                 _v_chunks(v, MLA_HEADS, MLA_VDIM, tile), tile)

    mixed = jnp.concatenate([_merge_heads(diff), _merge_heads(mla)], axis=-1).reshape(m, -1)
    return _mm(mixed, w_out.astype(BF16), res=x2).reshape(bsz, s_len, dm)


def kernel(x, norm_mix_g, norm_ffn_g, w_in_even, b_forget, w_out_even, w_in_odd, lambda_q1, lambda_k1, lambda_q2, lambda_k2, diff_subln_g, mla_q_norm_g, w_mla_uq, mla_kv_norm_g, w_mla_ukv, w_out_odd, t5_bias, w_ffn_gate, w_ffn_up, w_ffn_down, final_norm_g):
    bsz, s_len, dm = x.shape
    depth = norm_mix_g.shape[0]
    tile = _pick(s_len, (256, 128))
    x = x.astype(F32)
    for layer in range(depth):
        j = layer // 2
        if layer % 2 == 0:
            x = _even_mixer(x, norm_mix_g[layer], w_in_even[j], b_forget[j], w_out_even[j], t5_bias, tile)
        else:
            lambda_init = 0.8 - 0.6 * math.exp(-0.3 * layer)
            x = _odd_mixer(x, norm_mix_g[layer], w_in_odd[j], lambda_q1[j], lambda_k1[j], lambda_q2[j],
                           lambda_k2[j], diff_subln_g[j], mla_q_norm_g[j], w_mla_uq[j],
                           mla_kv_norm_g[j], w_mla_ukv[j], w_out_odd[j], t5_bias, lambda_init, tile)
        last = layer == depth - 1
        x = _ffn(x.reshape(bsz * s_len, dm), norm_ffn_g[layer], w_ffn_gate[layer].astype(BF16),
                 w_ffn_up[layer].astype(BF16), w_ffn_down[layer].astype(BF16),
                 final_g=final_norm_g if last else None).reshape(bsz, s_len, dm)
    return x
```

```python
import functools
import math

import numpy as np
import jax
import jax.numpy as jnp
from jax import lax
from jax.experimental import pallas as pl
from jax.experimental.pallas import tpu as pltpu

F32 = jnp.float32
BF16 = jnp.bfloat16

HEAD_DIM = 64
RMS_EPS = 1e-6
FOX_HEADS = 8
DSA_HEADS = 8
IDX_HEADS = 4
IDX_DIM = 64
DSA_TOPK = 256
DIFF_HEADS = 4
DIFF_VDIM = 2 * HEAD_DIM
MLA_HEADS = 8
MLA_NOPE = 64
MLA_ROPE = 32
MLA_VDIM = 64
MLA_Q_RANK = 256
MLA_KV_RANK = 128
ROPE_THETA = 10000.0
T5_BUCKETS = 32
T5_MAX_DIST = 128

FOX_W = FOX_HEADS * HEAD_DIM
DSA_W = DSA_HEADS * HEAD_DIM
EVEN_SPLITS = [FOX_W, FOX_W, FOX_W, FOX_HEADS, DSA_W, DSA_W, DSA_W,
               IDX_HEADS * IDX_DIM, IDX_DIM, IDX_HEADS]
DIFF_QK_W = DIFF_HEADS * 2 * HEAD_DIM
DIFF_V_W = DIFF_HEADS * DIFF_VDIM
ODD_SPLITS = [DIFF_QK_W, DIFF_QK_W, DIFF_V_W, MLA_Q_RANK, MLA_KV_RANK, MLA_ROPE]

LANES = 128
NEG = -1e30
INT_MIN = -(2 ** 31)
VMEM_LIMIT = 48 * 1024 * 1024


def _cparams(sem):
    return pltpu.CompilerParams(dimension_semantics=sem, vmem_limit_bytes=VMEM_LIMIT)


def _pick(n, prefs):
    for p in prefs:
        if n % p == 0:
            return p
    return n


def _mm_kernel(*refs, has_norm, has_res):
    it = iter(refs)
    x_ref = next(it)
    g_ref = next(it) if has_norm else None
    w_ref = next(it)
    r_ref = next(it) if has_res else None
    o_ref = next(it)
    xn_ref = next(it)

    @pl.when(pl.program_id(1) == 0)
    def _():
        x = x_ref[...].astype(F32)
        if has_norm:
            ms = jnp.mean(x * x, axis=-1, keepdims=True)
            x = x * lax.rsqrt(ms + RMS_EPS) * g_ref[...]
        xn_ref[...] = x.astype(BF16)

    acc = jnp.dot(xn_ref[...], w_ref[...], preferred_element_type=F32)
    if has_res:
        acc = acc + r_ref[...]
    o_ref[...] = acc.astype(o_ref.dtype)


def _mm(x, w, g=None, res=None, out_dtype=F32):
    m, k = x.shape
    n = w.shape[1]
    tm = _pick(m, (512, 256, 128))
    tn = _pick(n, (1152, 1024, 768, 512, 384, 256, 128))
    args = [x]
    specs = [pl.BlockSpec((tm, k), lambda i, j: (i, 0))]
    if g is not None:
        args.append(g.reshape(1, k).astype(F32))
        specs.append(pl.BlockSpec((1, k), lambda i, j: (0, 0)))
    args.append(w)
    specs.append(pl.BlockSpec((k, tn), lambda i, j: (0, j)))
    if res is not None:
        args.append(res)
        specs.append(pl.BlockSpec((tm, tn), lambda i, j: (i, j)))
    return pl.pallas_call(
        functools.partial(_mm_kernel, has_norm=g is not None, has_res=res is not None),
        grid=(m // tm, n // tn),
        in_specs=specs,
        out_specs=pl.BlockSpec((tm, tn), lambda i, j: (i, j)),
        out_shape=jax.ShapeDtypeStruct((m, n), out_dtype),
        scratch_shapes=[pltpu.VMEM((tm, k), BF16)],
        compiler_params=_cparams(("parallel", "arbitrary")),
        name="mm",
    )(*args)


def _ffn_kernel(*refs, has_final):
    it = iter(refs)
    x_ref, g_ref, wg_ref, wu_ref, wd_ref = (next(it) for _ in range(5))
    fg_ref = next(it) if has_final else None
    o_ref, hn_ref, acc_ref = next(it), next(it), next(it)
    f = pl.program_id(1)

    @pl.when(f == 0)
    def _():
        x = x_ref[...]
        ms = jnp.mean(x * x, axis=-1, keepdims=True)
        hn_ref[...] = (x * lax.rsqrt(ms + RMS_EPS) * g_ref[...]).astype(BF16)
        acc_ref[...] = jnp.zeros_like(acc_ref)

    h = hn_ref[...]
    gate = jnp.dot(h, wg_ref[...], preferred_element_type=F32)
    up = jnp.dot(h, wu_ref[...], preferred_element_type=F32)
    act = gate * jax.nn.sigmoid(gate) * up
    acc_ref[...] += jnp.dot(act.astype(BF16), wd_ref[...], preferred_element_type=F32)

    @pl.when(f == pl.num_programs(1) - 1)
    def _():
        y = x_ref[...] + acc_ref[...]
        if has_final:
            ms = jnp.mean(y * y, axis=-1, keepdims=True)
            y = y * lax.rsqrt(ms + RMS_EPS) * fg_ref[...]
        o_ref[...] = y


def _ffn(x, g, wg, wu, wd, final_g=None):
    m, d = x.shape
    hid = wg.shape[1]
    tm = _pick(m, (512, 256, 128))
    tf = _pick(hid, (256, 128))
    args = [x, g.reshape(1, d).astype(F32), wg, wu, wd]
    specs = [pl.BlockSpec((tm, d), lambda i, f: (i, 0)),
             pl.BlockSpec((1, d), lambda i, f: (0, 0)),
             pl.BlockSpec((d, tf), lambda i, f: (0, f)),
             pl.BlockSpec((d, tf), lambda i, f: (0, f)),
             pl.BlockSpec((tf, d), lambda i, f: (f, 0))]
    if final_g is not None:
        args.append(final_g.reshape(1, d).astype(F32))
        specs.append(pl.BlockSpec((1, d), lambda i, f: (0, 0)))
    return pl.pallas_call(
        functools.partial(_ffn_kernel, has_final=final_g is not None),
        grid=(m // tm, hid // tf),
        in_specs=specs,
        out_specs=pl.BlockSpec((tm, d), lambda i, f: (i, 0)),
        out_shape=jax.ShapeDtypeStruct((m, d), F32),
        scratch_shapes=[pltpu.VMEM((tm, d), BF16), pltpu.VMEM((tm, d), F32)],
        compiler_params=_cparams(("parallel", "arbitrary")),
        name="ffn",
    )(*args)


def _dsa_mask_kernel(qi_ref, ki_ref, w_ref, o_ref, keys_ref, cand_ref, *, tq, tk, nk, ksel, idx_scale):
    i = pl.program_id(1)
    groups = tk // LANES
    n_full = (i * tq) // tk
    off = i * tq - n_full * tk
    w = w_ref[0]
    kf = float(ksel)

    def score_chunk(j, diag):
        kc = ki_ref[0, j]
        sc = jnp.zeros((tq, tk), F32)
        for h in range(IDX_HEADS):
            d = jnp.dot(qi_ref[0, h], kc, preferred_element_type=F32)
            sc = sc + jnp.maximum(d, 0.0) * w[:, h:h + 1]
        sc = sc * idx_scale
        bits = lax.bitcast_convert_type(sc, jnp.int32)
        key = jnp.where(bits < 0, INT_MIN - bits, bits)
        if diag:
            row = lax.broadcasted_iota(jnp.int32, (tq, tk), 0)
            col = lax.broadcasted_iota(jnp.int32, (tq, tk), 1)
            key = jnp.where(col <= row + off, key, INT_MIN)
        keys_ref[j] = key

    def _plain(j, c):
        score_chunk(j, False)
        return c

    lax.fori_loop(0, n_full, _plain, 0)
    score_chunk(n_full, True)

    ones_l = jnp.ones((LANES, LANES), BF16)

    def count(strict):
        def body(j, acc):
            kc = keys_ref[j]
            cand = cand_ref[...]
            for g in range(groups):
                kg = kc[:, g * LANES:(g + 1) * LANES]
                acc = acc + jnp.where(kg > cand if strict else kg >= cand, 1.0, 0.0)
            return acc
        acc = lax.fori_loop(0, n_full + 1, body, jnp.zeros((tq, LANES), F32))
        return jnp.dot(acc.astype(BF16), ones_l, preferred_element_type=F32)

    cand_ref[...] = jnp.zeros((tq, LANES), jnp.int32)
    prefix = jnp.where(count(False) >= kf, 0, INT_MIN).astype(jnp.int32)

    def bit_body(b, prefix):
        cand = prefix + jnp.left_shift(jnp.int32(1), 30 - b)
        cand_ref[...] = cand
        return jnp.where(count(False) >= kf, cand, prefix)

    tau = lax.fori_loop(0, 31, bit_body, prefix)
    cand_ref[...] = tau
    need = kf - count(True)

    tri = (lax.broadcasted_iota(jnp.int32, (tk, tk), 0)
           <= lax.broadcasted_iota(jnp.int32, (tk, tk), 1)).astype(BF16)
    ones_t = jnp.ones((tk, LANES), BF16)
    tau_t = jnp.concatenate([tau] * groups, axis=1)
    need_t = jnp.concatenate([need] * groups, axis=1)

    def emit(j, carry):
        kc = keys_ref[j]
        eq = kc == tau_t
        eqf = jnp.where(eq, 1.0, 0.0).astype(BF16)
        rank = jnp.dot(eqf, tri, preferred_element_type=F32) + jnp.concatenate([carry] * groups, axis=1)
        sel = (kc > tau_t) | (eq & (rank <= need_t) & (kc > INT_MIN))
        o_ref[0, j] = jnp.where(sel, 0.0, NEG).astype(o_ref.dtype)
        return carry + jnp.dot(eqf, ones_t, preferred_element_type=F32)

    lax.fori_loop(0, n_full + 1, emit, jnp.zeros((tq, LANES), F32))

    def _fill(j, c):
        o_ref[0, j] = jnp.full((tq, tk), NEG, o_ref.dtype)
        return c

    lax.fori_loop(n_full + 1, nk, _fill, 0)


def _dsa_mask(qi3, ki3, w, tk, ksel):
    bsz, hi, s_len, d3 = qi3.shape
    nk = s_len // tk
    tq = _pick(tk, (256, 128))
    assert s_len // LANES <= 256, "per-lane counts must stay exact in bf16"
    idx_scale = (IDX_DIM ** -0.5) * (IDX_HEADS ** -0.5)
    return pl.pallas_call(
        functools.partial(_dsa_mask_kernel, tq=tq, tk=tk, nk=nk, ksel=ksel, idx_scale=idx_scale),
        grid=(bsz, s_len // tq),
        in_specs=[pl.BlockSpec((1, hi, tq, d3), lambda b, i: (b, 0, i, 0)),
                  pl.BlockSpec((1, nk, d3, tk), lambda b, i: (b, 0, 0, 0)),
                  pl.BlockSpec((1, tq, hi), lambda b, i: (b, i, 0))],
        out_specs=pl.BlockSpec((1, nk, tq, tk), lambda b, i: (b, 0, i, 0)),
        out_shape=jax.ShapeDtypeStruct((bsz, nk, s_len, tk), BF16),
        scratch_shapes=[pltpu.VMEM((nk, tq, tk), jnp.int32), pltpu.VMEM((tq, LANES), jnp.int32)],
        compiler_params=_cparams(("parallel", "arbitrary")),
        name="dsa_mask",
    )(qi3, ki3, w)


def _flash_kernel(*refs, tile, dv, q_axis, has_kbias, has_tiles, has_mask):
    it = iter(refs)
    q_ref, k_ref, v_ref = next(it), next(it), next(it)
    kb_ref = next(it) if has_kbias else None
    t_ref = next(it) if has_tiles else None
    mk_ref = next(it) if has_mask else None
    o_ref, m_ref, acc_ref = next(it), next(it), next(it)

    i = pl.program_id(q_axis)
    q = q_ref[0, 0]
    s_groups = tile // LANES
    a_groups = acc_ref.shape[-1] // LANES
    m_ref[...] = jnp.full_like(m_ref, NEG)
    acc_ref[...] = jnp.zeros_like(acc_ref)

    def step(j, rel_tile=None, diag=False):
        s = jnp.dot(q, k_ref[0, 0, j], preferred_element_type=F32)
        if has_kbias:
            s = s + kb_ref[0, 0, j]
        if rel_tile is not None:
            s = s + t_ref[0, rel_tile]
        if has_mask:
            s = s + mk_ref[0, j].astype(F32)
        elif diag:
            row = lax.broadcasted_iota(jnp.int32, (tile, tile), 0)
            col = lax.broadcasted_iota(jnp.int32, (tile, tile), 1)
            s = jnp.where(col <= row, s, NEG)
        m_old = m_ref[...]
        m_new = jnp.maximum(m_old, jnp.max(s, axis=-1, keepdims=True))
        p = jnp.exp(s - jnp.concatenate([m_new] * s_groups, axis=1))
        alpha = jnp.exp(m_old - m_new)
        acc_ref[...] = acc_ref[...] * jnp.concatenate([alpha] * a_groups, axis=1) + jnp.dot(
            p.astype(BF16), v_ref[0, 0, j], preferred_element_type=F32)
        m_ref[...] = m_new

    def _far(j, c):
        step(j)
        return c

    if has_tiles:
        lax.fori_loop(0, i - 1, _far, 0)

        @pl.when(i >= 1)
        def _():
            step(i - 1, rel_tile=1)

        step(i, rel_tile=0, diag=True)
    else:
        lax.fori_loop(0, i, _far, 0)
        step(i, diag=True)

    acc = acc_ref[...]
    o_ref[0, 0] = (acc[:, :dv] / acc[:, dv:dv + 1]).astype(o_ref.dtype)


def _flash(q, kt, v1, tile, dv, kbias=None, rel_tiles=None, mask=None, v_group=1):
    bsz, nh, s_len, dq = q.shape
    nk = s_len // tile
    dvp = v1.shape[-1]
    if mask is None:
        grid, q_axis = (bsz, nh, nk), 2
        ix = lambda f: (lambda b, h, i: f(b, h, i))
    else:
        grid, q_axis = (bsz, nk, nh), 1
        ix = lambda f: (lambda b, i, h: f(b, h, i))
    args = [q, kt, v1]
    specs = [pl.BlockSpec((1, 1, tile, dq), ix(lambda b, h, i: (b, h, i, 0))),
             pl.BlockSpec((1, 1, nk, dq, tile), ix(lambda b, h, i: (b, h, 0, 0, 0))),
             pl.BlockSpec((1, 1, nk, tile, dvp), ix(lambda b, h, i: (b, h // v_group, 0, 0, 0)))]
    if kbias is not None:
        args.append(kbias)
        specs.append(pl.BlockSpec((1, 1, nk, 1, tile), ix(lambda b, h, i: (b, h, 0, 0, 0))))
    if rel_tiles is not None:
        args.append(rel_tiles)
        specs.append(pl.BlockSpec((1, 2, tile, tile), ix(lambda b, h, i: (h, 0, 0, 0))))
    if mask is not None:
        args.append(mask)
        specs.append(pl.BlockSpec((1, nk, tile, tile), ix(lambda b, h, i: (b, 0, i, 0))))
    return pl.pallas_call(
        functools.partial(_flash_kernel, tile=tile, dv=dv, q_axis=q_axis, has_kbias=kbias is not None,
                          has_tiles=rel_tiles is not None, has_mask=mask is not None),
        grid=grid,
        in_specs=specs,
        out_specs=pl.BlockSpec((1, 1, tile, dv), ix(lambda b, h, i: (b, h, i, 0))),
        out_shape=jax.ShapeDtypeStruct((bsz, nh, s_len, dv), F32),
        scratch_shapes=[pltpu.VMEM((tile, LANES), F32), pltpu.VMEM((tile, dvp), F32)],
        compiler_params=_cparams(("parallel", "parallel", "arbitrary")),
        name="flash",
    )(*args)


def _t5_bucket_map(tile):
    exact = T5_BUCKETS // 2
    r = np.arange(tile)[:, None]
    c = np.arange(tile)[None, :]
    out = []
    for off in (0, tile):
        dist = np.maximum(off + r - c, 0)
        d = np.maximum(dist, 1).astype(np.float32)
        log_b = exact + (np.log(d / np.float32(exact)) / np.float32(math.log(T5_MAX_DIST / exact))
                         * np.float32(T5_BUCKETS - exact)).astype(np.int32)
        log_b = np.minimum(log_b, T5_BUCKETS - 1)
        out.append(np.where(dist < exact, dist, log_b))
    return np.stack(out).astype(np.int32)


def _rel_tiles(t5_table, tile):
    assert tile >= T5_MAX_DIST
    bmap = _t5_bucket_map(tile)
    t = t5_table.astype(F32)
    tiles = t[bmap] - t[T5_BUCKETS - 1]
    return jnp.transpose(tiles, (3, 0, 1, 2))


def _split_cols(y, sizes):
    offs = np.cumsum([0] + list(sizes))
    return [y[..., int(offs[k]):int(offs[k + 1])] for k in range(len(sizes))]


def _q_heads(a, n, d, scale):
    b, s, _ = a.shape
    return jnp.transpose((a * scale).reshape(b, s, n, d), (0, 2, 1, 3)).astype(BF16)


def _kt_chunks(a, n, d, tile):
    b, s, _ = a.shape
    return jnp.transpose(a.reshape(b, s // tile, tile, n, d), (0, 3, 1, 4, 2)).astype(BF16)


def _v_chunks(a, n, d, tile):
    b, s, _ = a.shape
    a = a.reshape(b, s // tile, tile, n, d)
    dvp = -(-(d + 1) // LANES) * LANES
    a = jnp.concatenate([a, jnp.ones(a.shape[:-1] + (1,), a.dtype),
                         jnp.zeros(a.shape[:-1] + (dvp - d - 1,), a.dtype)], axis=-1)
    return jnp.transpose(a, (0, 3, 1, 2, 4)).astype(BF16)


def _merge_heads(o):
    b, n, s, d = o.shape
    return jnp.transpose(o, (0, 2, 1, 3)).reshape(b, s, n * d)


def _hi_lo(a):
    hi = a.astype(BF16)
    lo = (a - hi.astype(F32)).astype(BF16)
    return hi, lo


def _pad_cols(w, mult=LANES):
    n = w.shape[-1]
    pad = (-n) % mult
    return jnp.pad(w, ((0, 0), (0, pad))) if pad else w


def _rope_tables(s_len):
    pos = jnp.arange(s_len, dtype=F32)
    inv = ROPE_THETA ** (-jnp.arange(0, MLA_ROPE, 2, dtype=F32) / MLA_ROPE)
    ang = pos[:, None] * inv[None, :]
    ang = jnp.concatenate([ang, ang], axis=-1)
    return jnp.cos(ang), jnp.sin(ang)


def _apply_rope(x, cos, sin):
    x1, x2 = jnp.split(x, 2, axis=-1)
    rot = jnp.concatenate([-x2, x1], axis=-1)
    return x * cos + rot * sin


def _even_mixer(x, g, w_in, b_forget, w_out, t5_table, tile):
    bsz, s_len, dm = x.shape
    m = bsz * s_len
    x2 = x.reshape(m, dm)
    y = _mm(x2, _pad_cols(w_in).astype(BF16), g=g).reshape(bsz, s_len, -1)
    fq, fk, fv, ff, dq, dk, dv, iq, ik, iw = _split_cols(y[..., :sum(EVEN_SPLITS)], EVEN_SPLITS)
    nk = s_len // tile
    scale = HEAD_DIM ** -0.5

    log_f = jax.nn.log_sigmoid(ff + b_forget.astype(F32))
    log_cum = jnp.cumsum(log_f, axis=1)
    kbias = (-jnp.transpose(log_cum, (0, 2, 1))).reshape(bsz, FOX_HEADS, nk, 1, tile)
    fox = _flash(_q_heads(fq, FOX_HEADS, HEAD_DIM, scale), _kt_chunks(fk, FOX_HEADS, HEAD_DIM, tile),
                 _v_chunks(fv, FOX_HEADS, HEAD_DIM, tile), tile, HEAD_DIM, kbias=kbias)

    q_hi, q_lo = _hi_lo(iq.reshape(bsz, s_len, IDX_HEADS, IDX_DIM))
    k_hi, k_lo = _hi_lo(ik)
    qi3 = jnp.transpose(jnp.concatenate([q_hi, q_hi, q_lo], axis=-1), (0, 2, 1, 3))
    ki3 = jnp.transpose(jnp.concatenate([k_hi, k_lo, k_hi], axis=-1).reshape(bsz, nk, tile, 3 * IDX_DIM),
                        (0, 1, 3, 2))
    mask = _dsa_mask(qi3, ki3, iw, tile, min(DSA_TOPK, s_len // 4))
    dsa = _flash(_q_heads(dq, DSA_HEADS, HEAD_DIM, scale), _kt_chunks(dk, DSA_HEADS, HEAD_DIM, tile),
                 _v_chunks(dv, DSA_HEADS, HEAD_DIM, tile), tile, HEAD_DIM,
                 rel_tiles=_rel_tiles(t5_table, tile), mask=mask)

    mixed = jnp.concatenate([_merge_heads(fox), _merge_heads(dsa)], axis=-1).reshape(m, -1)
    return _mm(mixed, w_out.astype(BF16), res=x2).reshape(bsz, s_len, dm)


def _odd_mixer(x, g, w_in, lq1, lk1, lq2, lk2, subln_g, q_norm_g, w_uq, kv_norm_g, w_ukv,
               w_out, t5_table, lambda_init, tile):
    bsz, s_len, dm = x.shape
    m = bsz * s_len
    x2 = x.reshape(m, dm)
    y = _mm(x2, _pad_cols(w_in).astype(BF16), g=g)
    cq, ck, cv, mcq, mckv, mkr = _split_cols(y[:, :sum(ODD_SPLITS)], ODD_SPLITS)
    r3 = lambda a: a.reshape(bsz, s_len, -1)

    lam = (jnp.exp(jnp.sum(lq1.astype(F32) * lk1.astype(F32)))
           - jnp.exp(jnp.sum(lq2.astype(F32) * lk2.astype(F32))) + lambda_init)
    maps = _flash(_q_heads(r3(cq), 2 * DIFF_HEADS, HEAD_DIM, HEAD_DIM ** -0.5),
                  _kt_chunks(r3(ck), 2 * DIFF_HEADS, HEAD_DIM, tile),
                  _v_chunks(r3(cv), DIFF_HEADS, DIFF_VDIM, tile), tile, DIFF_VDIM,
                  rel_tiles=_rel_tiles(t5_table, tile), v_group=2)
    maps = maps.reshape(bsz, DIFF_HEADS, 2, s_len, DIFF_VDIM)
    diff = maps[:, :, 0] - lam * maps[:, :, 1]
    ms = jnp.mean(diff * diff, axis=-1, keepdims=True)
    diff = diff * lax.rsqrt(ms + RMS_EPS) * subln_g.astype(F32) * (1.0 - lambda_init)

    q = _mm(mcq, w_uq.astype(BF16), g=q_norm_g).reshape(bsz, s_len, MLA_HEADS, MLA_NOPE + MLA_ROPE)
    kv = _mm(mckv, w_ukv.astype(BF16), g=kv_norm_g).reshape(bsz, s_len, MLA_HEADS, MLA_NOPE + MLA_VDIM)
    cos, sin = _rope_tables(s_len)
    q_rope = _apply_rope(q[..., MLA_NOPE:], cos[:, None, :], sin[:, None, :])
    k_rope = _apply_rope(r3(mkr), cos, sin)
    dqk = MLA_NOPE + MLA_ROPE
    q_cat = jnp.concatenate([q[..., :MLA_NOPE], q_rope], axis=-1).reshape(bsz, s_len, MLA_HEADS * dqk)
    k_cat = jnp.concatenate(
        [kv[..., :MLA_NOPE], jnp.broadcast_to(k_rope[:, :, None, :], (bsz, s_len, MLA_HEADS, MLA_ROPE))],
        axis=-1).reshape(bsz, s_len, MLA_HEADS * dqk)
    v = kv[..., MLA_NOPE:].reshape(bsz, s_len, MLA_HEADS * MLA_VDIM)
    mla = _flash(_q_heads(q_cat, MLA_HEADS, dqk, dqk ** -0.5), _kt_chunks(k_cat, MLA_HEADS, dqk, tile),
                 _v_chunks(v, MLA_HEADS, MLA_VDIM, tile), tile, MLA_VDIM)

    mixed = jnp.concatenate([_merge_heads(diff), _merge_heads(mla)], axis=-1).reshape(m, -1)
    return _mm(mixed, w_out.astype(BF16), res=x2).reshape(bsz, s_len, dm)


def kernel(x, norm_mix_g, norm_ffn_g, w_in_even, b_forget, w_out_even, w_in_odd, lambda_q1, lambda_k1, lambda_q2, lambda_k2, diff_subln_g, mla_q_norm_g, w_mla_uq, mla_kv_norm_g, w_mla_ukv, w_out_odd, t5_bias, w_ffn_gate, w_ffn_up, w_ffn_down, final_norm_g):
    bsz, s_len, dm = x.shape
    depth = norm_mix_g.shape[0]
    tile = _pick(s_len, (512, 256, 128))
    x = x.astype(F32)
    for layer in range(depth):
        j = layer // 2
        if layer % 2 == 0:
            x = _even_mixer(x, norm_mix_g[layer], w_in_even[j], b_forget[j], w_out_even[j], t5_bias, tile)
        else:
            lambda_init = 0.8 - 0.6 * math.exp(-0.3 * layer)
            x = _odd_mixer(x, norm_mix_g[layer], w_in_odd[j], lambda_q1[j], lambda_k1[j], lambda_q2[j],
                           lambda_k2[j], diff_subln_g[j], mla_q_norm_g[j], w_mla_uq[j],
                           mla_kv_norm_g[j], w_mla_ukv[j], w_out_odd[j], t5_bias, lambda_init, tile)
        last = layer == depth - 1
        x = _ffn(x.reshape(bsz * s_len, dm), norm_ffn_g[layer], w_ffn_gate[layer].astype(BF16),
                 w_ffn_up[layer].astype(BF16), w_ffn_down[layer].astype(BF16),
                 final_g=final_norm_g if last else None).reshape(bsz, s_len, dm)
    return x
```

```python
import functools
import math

import numpy as np
import jax
import jax.numpy as jnp
from jax import lax
from jax.experimental import pallas as pl
from jax.experimental.pallas import tpu as pltpu

F32 = jnp.float32
BF16 = jnp.bfloat16

HEAD_DIM = 64
RMS_EPS = 1e-6
FOX_HEADS = 8
DSA_HEADS = 8
IDX_HEADS = 4
IDX_DIM = 64
DSA_TOPK = 256
DIFF_HEADS = 4
DIFF_VDIM = 2 * HEAD_DIM
MLA_HEADS = 8
MLA_NOPE = 64
MLA_ROPE = 32
MLA_VDIM = 64
MLA_Q_RANK = 256
MLA_KV_RANK = 128
ROPE_THETA = 10000.0
T5_BUCKETS = 32
T5_MAX_DIST = 128

FOX_W = FOX_HEADS * HEAD_DIM
DSA_W = DSA_HEADS * HEAD_DIM
EVEN_SPLITS = [FOX_W, FOX_W, FOX_W, FOX_HEADS, DSA_W, DSA_W, DSA_W,
               IDX_HEADS * IDX_DIM, IDX_DIM, IDX_HEADS]
DIFF_QK_W = DIFF_HEADS * 2 * HEAD_DIM
DIFF_V_W = DIFF_HEADS * DIFF_VDIM
ODD_SPLITS = [DIFF_QK_W, DIFF_QK_W, DIFF_V_W, MLA_Q_RANK, MLA_KV_RANK, MLA_ROPE]

LANES = 128
NEG = -1e30
INT_MIN = -(2 ** 31)
VMEM_LIMIT = 48 * 1024 * 1024
FAR_WIDTH = 4
LOG2E = math.log2(math.e)


def _cparams(sem):
    return pltpu.CompilerParams(dimension_semantics=sem, vmem_limit_bytes=VMEM_LIMIT)


def _pick(n, prefs):
    for p in prefs:
        if n % p == 0:
            return p
    return n


def _mm_kernel(*refs, n_norm, has_g, has_w2, has_rot, has_bias, has_res):
    it = iter(refs)
    x_ref = next(it)
    g_ref = next(it) if has_g else None
    w_ref = next(it)
    w2_ref = next(it) if has_w2 else None
    c1_ref = next(it) if has_rot else None
    c2_ref = next(it) if has_rot else None
    b_ref = next(it) if has_bias else None
    r_ref = next(it) if has_res else None
    o_ref = next(it)
    xn_ref = next(it)

    @pl.when(pl.program_id(1) == 0)
    def _():
        x = x_ref[...].astype(F32)
        if has_g:
            kb = x.shape[-1]
            if n_norm == kb:
                ms = jnp.mean(x * x, axis=-1, keepdims=True)
                x = x * lax.rsqrt(ms + RMS_EPS) * g_ref[...]
            else:
                normed = lax.broadcasted_iota(jnp.int32, x.shape, 1) < n_norm
                xs = jnp.where(normed, x, 0.0)
                ms = jnp.sum(xs * xs, axis=-1, keepdims=True) * (1.0 / n_norm)
                x = x * jnp.where(normed, lax.rsqrt(ms + RMS_EPS), 1.0) * g_ref[...]
        xn_ref[...] = x.astype(BF16)

    xn = xn_ref[...]
    acc = jnp.dot(xn, w_ref[...], preferred_element_type=F32)
    if has_rot:
        reps = acc.shape[-1] // LANES
        acc = acc * jnp.concatenate([c1_ref[...]] * reps, axis=1)
        acc = acc + jnp.dot(xn, w2_ref[...], preferred_element_type=F32) * jnp.concatenate(
            [c2_ref[...]] * reps, axis=1)
    if has_bias:
        acc = acc + b_ref[...]
    if has_res:
        acc = acc + r_ref[...]
    o_ref[...] = acc.astype(o_ref.dtype)


def _mm(x, w, *, xblk=0, kb=None, g=None, n_norm=None, w2=None, rot=None, bias=None, res=None,
        out_dtype=F32):
    m = x.shape[0]
    kb = x.shape[1] if kb is None else kb
    n = w.shape[1]
    tm = _pick(m, (512, 256, 128))
    tn = _pick(n, (1024, 768, 512, 384, 256, 128))
    args = [x]
    specs = [pl.BlockSpec((tm, kb), lambda i, j: (i, xblk))]
    if g is not None:
        n_norm = kb if n_norm is None else n_norm
        args.append(g.reshape(1, kb).astype(F32))
        specs.append(pl.BlockSpec((1, kb), lambda i, j: (0, 0)))
    args.append(w)
    specs.append(pl.BlockSpec((kb, tn), lambda i, j: (0, j)))
    if w2 is not None:
        args.append(w2)
        specs.append(pl.BlockSpec((kb, tn), lambda i, j: (0, j)))
    if rot is not None:
        s_len = rot[0].shape[0]
        assert s_len % tm == 0
        r = rot[0].shape[1] // LANES
        for c in rot:
            args.append(c)
            specs.append(pl.BlockSpec((tm, LANES), lambda i, j: (i % (s_len // tm), j % r)))
    if bias is not None:
        args.append(bias.reshape(1, n).astype(F32))
        specs.append(pl.BlockSpec((1, tn), lambda i, j: (0, j)))
    if res is not None:
        args.append(res)
        specs.append(pl.BlockSpec((tm, tn), lambda i, j: (i, j)))
    return pl.pallas_call(
        functools.partial(_mm_kernel, n_norm=n_norm, has_g=g is not None, has_w2=w2 is not None,
                          has_rot=rot is not None, has_bias=bias is not None, has_res=res is not None),
        grid=(m // tm, n // tn),
        in_specs=specs,
        out_specs=pl.BlockSpec((tm, tn), lambda i, j: (i, j)),
        out_shape=jax.ShapeDtypeStruct((m, n), out_dtype),
        scratch_shapes=[pltpu.VMEM((tm, kb), BF16)],
        compiler_params=_cparams(("parallel", "arbitrary")),
        name="mm",
    )(*args)


def _ffn_kernel(*refs, has_final):
    it = iter(refs)
    x_ref, g_ref, wg_ref, wu_ref, wd_ref = (next(it) for _ in range(5))
    fg_ref = next(it) if has_final else None
    o_ref, hn_ref, acc_ref = next(it), next(it), next(it)
    f = pl.program_id(1)

    @pl.when(f == 0)
    def _():
        x = x_ref[...]
        ms = jnp.mean(x * x, axis=-1, keepdims=True)
        hn_ref[...] = (x * lax.rsqrt(ms + RMS_EPS) * g_ref[...]).astype(BF16)
        acc_ref[...] = jnp.zeros_like(acc_ref)

    h = hn_ref[...]
    gate = jnp.dot(h, wg_ref[...], preferred_element_type=F32)
    up = jnp.dot(h, wu_ref[...], preferred_element_type=F32)
    act = gate * jax.nn.sigmoid(gate) * up
    acc_ref[...] += jnp.dot(act.astype(BF16), wd_ref[...], preferred_element_type=F32)

    @pl.when(f == pl.num_programs(1) - 1)
    def _():
        y = x_ref[...] + acc_ref[...]
        if has_final:
            ms = jnp.mean(y * y, axis=-1, keepdims=True)
            y = y * lax.rsqrt(ms + RMS_EPS) * fg_ref[...]
        o_ref[...] = y


def _ffn(x, g, wg, wu, wd, final_g=None):
    m, d = x.shape
    hid = wg.shape[1]
    tm = _pick(m, (512, 256, 128))
    tf = _pick(hid, (256, 128))
    args = [x, g.reshape(1, d).astype(F32), wg, wu, wd]
    specs = [pl.BlockSpec((tm, d), lambda i, f: (i, 0)),
             pl.BlockSpec((1, d), lambda i, f: (0, 0)),
             pl.BlockSpec((d, tf), lambda i, f: (0, f)),
             pl.BlockSpec((d, tf), lambda i, f: (0, f)),
             pl.BlockSpec((tf, d), lambda i, f: (f, 0))]
    if final_g is not None:
        args.append(final_g.reshape(1, d).astype(F32))
        specs.append(pl.BlockSpec((1, d), lambda i, f: (0, 0)))
    return pl.pallas_call(
        functools.partial(_ffn_kernel, has_final=final_g is not None),
        grid=(m // tm, hid // tf),
        in_specs=specs,
        out_specs=pl.BlockSpec((tm, d), lambda i, f: (i, 0)),
        out_shape=jax.ShapeDtypeStruct((m, d), F32),
        scratch_shapes=[pltpu.VMEM((tm, d), BF16), pltpu.VMEM((tm, d), F32)],
        compiler_params=_cparams(("parallel", "arbitrary")),
        name="ffn",
    )(*args)


def _dsa_mask_kernel(qi_ref, ki_ref, w_ref, o_ref, keys_ref, cand_ref, *, tq, tk, nk, ksel, idx_scale):
    i = pl.program_id(1)
    groups = tk // LANES
    n_full = (i * tq) // tk
    off = i * tq - n_full * tk
    w = w_ref[0]
    kf = float(ksel)

    def score_chunk(j, diag):
        kc = ki_ref[0, j]
        sc = jnp.zeros((tq, tk), F32)
        for h in range(IDX_HEADS):
            d = jnp.dot(qi_ref[0, h], kc, preferred_element_type=F32)
            sc = sc + jnp.maximum(d, 0.0) * w[:, h:h + 1]
        sc = sc * idx_scale
        bits = lax.bitcast_convert_type(sc, jnp.int32)
        key = jnp.where(bits < 0, INT_MIN - bits, bits)
        if diag:
            row = lax.broadcasted_iota(jnp.int32, (tq, tk), 0)
            col = lax.broadcasted_iota(jnp.int32, (tq, tk), 1)
            key = jnp.where(col <= row + off, key, INT_MIN)
        keys_ref[j] = key

    def _plain(j, c):
        score_chunk(j, False)
        return c

    lax.fori_loop(0, n_full, _plain, 0)
    score_chunk(n_full, True)

    ones_l = jnp.ones((LANES, LANES), BF16)

    def count(strict):
        def body(j, acc):
            kc = keys_ref[j]
            cand = cand_ref[...]
            for g in range(groups):
                kg = kc[:, g * LANES:(g + 1) * LANES]
                acc = acc + jnp.where(kg > cand if strict else kg >= cand, 1.0, 0.0)
            return acc
        acc = lax.fori_loop(0, n_full + 1, body, jnp.zeros((tq, LANES), F32))
        return jnp.dot(acc.astype(BF16), ones_l, preferred_element_type=F32)

    cand_ref[...] = jnp.zeros((tq, LANES), jnp.int32)
    prefix = jnp.where(count(False) >= kf, 0, INT_MIN).astype(jnp.int32)

    def bit_body(b, prefix):
        cand = prefix + jnp.left_shift(jnp.int32(1), 30 - b)
        cand_ref[...] = cand
        return jnp.where(count(False) >= kf, cand, prefix)

    tau = lax.fori_loop(0, 31, bit_body, prefix)
    cand_ref[...] = tau
    need = kf - count(True)

    tri = (lax.broadcasted_iota(jnp.int32, (tk, tk), 0)
           <= lax.broadcasted_iota(jnp.int32, (tk, tk), 1)).astype(BF16)
    ones_t = jnp.ones((tk, LANES), BF16)
    tau_t = jnp.concatenate([tau] * groups, axis=1)
    need_t = jnp.concatenate([need] * groups, axis=1)

    def emit(j, carry):
        kc = keys_ref[j]
        eq = kc == tau_t
        eqf = jnp.where(eq, 1.0, 0.0).astype(BF16)
        rank = jnp.dot(eqf, tri, preferred_element_type=F32) + jnp.concatenate([carry] * groups, axis=1)
        sel = (kc > tau_t) | (eq & (rank <= need_t) & (kc > INT_MIN))
        o_ref[0, j] = jnp.where(sel, 0.0, NEG).astype(o_ref.dtype)
        return carry + jnp.dot(eqf, ones_t, preferred_element_type=F32)

    lax.fori_loop(0, n_full + 1, emit, jnp.zeros((tq, LANES), F32))

    def _fill(j, c):
        o_ref[0, j] = jnp.full((tq, tk), NEG, o_ref.dtype)
        return c

    lax.fori_loop(n_full + 1, nk, _fill, 0)


def _dsa_mask(qi3, ki3, w, tk, ksel):
    bsz, hi, s_len, d3 = qi3.shape
    nk = s_len // tk
    tq = _pick(tk, (256, 128))
    assert s_len // LANES <= 256, "per-lane counts must stay exact in bf16"
    idx_scale = (IDX_DIM ** -0.5) * (IDX_HEADS ** -0.5)
    return pl.pallas_call(
        functools.partial(_dsa_mask_kernel, tq=tq, tk=tk, nk=nk, ksel=ksel, idx_scale=idx_scale),
        grid=(bsz, s_len // tq),
        in_specs=[pl.BlockSpec((1, hi, tq, d3), lambda b, i: (b, 0, i, 0)),
                  pl.BlockSpec((1, nk, d3, tk), lambda b, i: (b, 0, 0, 0)),
                  pl.BlockSpec((1, tq, hi), lambda b, i: (b, i, 0))],
        out_specs=pl.BlockSpec((1, nk, tq, tk), lambda b, i: (b, 0, i, 0)),
        out_shape=jax.ShapeDtypeStruct((bsz, nk, s_len, tk), BF16),
        scratch_shapes=[pltpu.VMEM((nk, tq, tk), jnp.int32), pltpu.VMEM((tq, LANES), jnp.int32)],
        compiler_params=_cparams(("parallel", "arbitrary")),
        name="dsa_mask",
    )(qi3, ki3, w)


def _add_band(s, band, near):
    nb = s.shape[0] // LANES
    rows = []
    for a in range(nb):
        blocks = []
        for b in range(nb):
            blk = s[a * LANES:(a + 1) * LANES, b * LANES:(b + 1) * LANES]
            if near == 0 and b == a:
                blk = blk + band[0]
            elif (near == 0 and b == a - 1) or (near == 1 and a == 0 and b == nb - 1):
                blk = blk + band[1]
            blocks.append(blk)
        rows.append(jnp.concatenate(blocks, axis=1))
    return jnp.concatenate(rows, axis=0)


def _flash_kernel(*refs, tile, wq, dv, v_shared, q_axis, has_kbias, has_band, has_mask, diff):
    it = iter(refs)
    q_ref, k_ref, v_ref = next(it), next(it), next(it)
    kb_ref = next(it) if has_kbias else None
    t_ref = next(it) if has_band else None
    mk_ref = next(it) if has_mask else None
    lam_ref = next(it) if diff else None
    sg_ref = next(it) if diff else None
    o_ref, m_ref, acc_ref = next(it), next(it), next(it)

    i = pl.program_id(q_axis)
    dvp = acc_ref.shape[-1]
    a_groups = dvp // LANES
    qblk = q_ref[0]
    if wq == LANES:
        lane = lax.broadcasted_iota(jnp.int32, qblk.shape, 1)
        q32 = qblk.astype(F32)
        qs = [jnp.where(lane < LANES // 2, q32, 0.0).astype(BF16),
              jnp.where(lane >= LANES // 2, q32, 0.0).astype(BF16)]
    else:
        qs = [qblk[:, :LANES], qblk[:, LANES:]]
    m_ref[...] = jnp.full_like(m_ref, NEG)
    acc_ref[...] = jnp.zeros_like(acc_ref)

    def step(j, width=1, near=None, diag=False):
        cols = width * tile
        off = pl.multiple_of(j * tile, tile)
        kc = k_ref[0, pl.ds(off, cols), :]
        vc = v_ref[0, pl.ds(off, cols), :]
        cat = lambda f: f(j) if width == 1 else jnp.concatenate([f(j + t) for t in range(width)], axis=1)
        mk = cat(lambda t: mk_ref[0, t]).astype(F32) if has_mask else None
        for a in range(2):
            ka = kc if wq == LANES else kc[:, a * LANES:(a + 1) * LANES]
            s = lax.dot_general(qs[a], ka, (((1,), (1,)), ((), ())), preferred_element_type=F32)
            if has_kbias:
                s = s + cat(lambda t: kb_ref[0, a, t])
            if near is not None:
                s = _add_band(s, (t_ref[a, 0], t_ref[a, 1]), near)
            if has_mask:
                s = s + mk
            elif diag:
                row = lax.broadcasted_iota(jnp.int32, (tile, tile), 0)
                col = lax.broadcasted_iota(jnp.int32, (tile, tile), 1)
                s = jnp.where(col <= row, s, NEG)
            m_old = m_ref[a]
            m_new = jnp.maximum(m_old, jnp.max(s, axis=-1, keepdims=True))
            p = jnp.exp2(s - jnp.concatenate([m_new] * (cols // LANES), axis=1))
            alpha = jnp.exp2(m_old - m_new)
            va = vc if v_shared else vc[:, a * LANES:(a + 1) * LANES]
            acc_ref[a] = acc_ref[a] * jnp.concatenate([alpha] * a_groups, axis=1) + jnp.dot(
                p.astype(BF16), va, preferred_element_type=F32)
            m_ref[a] = m_new

    n_far = jnp.maximum(i - 1, 0) if has_band else i

    def _far(t, c):
        step(t * FAR_WIDTH, width=FAR_WIDTH)
        return c

    lax.fori_loop(0, n_far // FAR_WIDTH, _far, 0)
    w = FAR_WIDTH // 2
    while w >= 1:
        @pl.when(n_far % (2 * w) >= w)
        def _(w=w):
            step((n_far // (2 * w)) * (2 * w), width=w)
        w //= 2

    if has_band:
        @pl.when(i >= 1)
        def _():
            step(i - 1, near=1)

        step(i, near=0, diag=True)
    else:
        step(i, diag=True)

    outs = []
    for a in range(2):
        acc = acc_ref[a]
        outs.append(acc[:, :dv] / acc[:, dv:dv + 1])
    if diff:
        d = outs[0] - lam_ref[...] * outs[1]
        ms = jnp.mean(d * d, axis=-1, keepdims=True)
        o_ref[0] = (d * lax.rsqrt(ms + RMS_EPS) * sg_ref[...]).astype(o_ref.dtype)
    else:
        o_ref[0] = jnp.concatenate(outs, axis=1).astype(o_ref.dtype)


def _flash(qa, ka, va, *, n_pairs, qoff, koff, voff, wq, dv, tile, v_shared=False, kbias=None,
           band=None, mask=None, diff=None):
    bsz, s_len, _ = qa.shape
    nk = s_len // tile
    dvp = 2 * LANES if v_shared else LANES
    assert qoff % wq == 0 and koff % wq == 0 and voff % (2 * LANES) == 0
    qb, kb_, vb = qoff // wq, koff // wq, voff // (2 * LANES)
    if mask is None:
        grid, q_axis = (bsz, n_pairs, nk), 2
        ix = lambda f: (lambda b, h, i: f(b, h, i))
    else:
        grid, q_axis = (bsz, nk, n_pairs), 1
        ix = lambda f: (lambda b, i, h: f(b, h, i))
    args = [qa, ka, va]
    specs = [pl.BlockSpec((1, tile, wq), ix(lambda b, h, i: (b, i, qb + h))),
             pl.BlockSpec((1, s_len, wq), ix(lambda b, h, i: (b, 0, kb_ + h))),
             pl.BlockSpec((1, s_len, 2 * LANES), ix(lambda b, h, i: (b, 0, vb + h)))]
    if kbias is not None:
        args.append(kbias)
        specs.append(pl.BlockSpec((1, 2, nk, 1, tile), ix(lambda b, h, i: (b, h, 0, 0, 0))))
    if band is not None:
        args.append(band)
        specs.append(pl.BlockSpec((2, 2, LANES, LANES), ix(lambda b, h, i: (h, 0, 0, 0))))
    if mask is not None:
        args.append(mask)
        specs.append(pl.BlockSpec((1, nk, tile, tile), ix(lambda b, h, i: (b, 0, i, 0))))
    if diff is not None:
        for t in diff:
            args.append(t.reshape(1, LANES).astype(F32))
            specs.append(pl.BlockSpec((1, LANES), ix(lambda b, h, i: (0, 0))))
    return pl.pallas_call(
        functools.partial(_flash_kernel, tile=tile, wq=wq, dv=dv, v_shared=v_shared, q_axis=q_axis,
                          has_kbias=kbias is not None, has_band=band is not None,
                          has_mask=mask is not None, diff=diff is not None),
        grid=grid,
        in_specs=specs,
        out_specs=pl.BlockSpec((1, tile, LANES), ix(lambda b, h, i: (b, i, h))),
        out_shape=jax.ShapeDtypeStruct((bsz, s_len, n_pairs * LANES), BF16),
        scratch_shapes=[pltpu.VMEM((2, tile, LANES), F32), pltpu.VMEM((2, tile, dvp), F32)],
        compiler_params=_cparams(("parallel", "parallel", "arbitrary")),
        name="flash",
    )(*args)


def _t5_bucket(dist):
    exact = T5_BUCKETS // 2
    d = np.maximum(dist, 1).astype(np.float32)
    log_b = exact + (np.log(d / np.float32(exact)) / np.float32(math.log(T5_MAX_DIST / exact))
                     * np.float32(T5_BUCKETS - exact)).astype(np.int32)
    log_b = np.minimum(log_b, T5_BUCKETS - 1)
    return np.where(dist < exact, dist, log_b).astype(np.int32)


def _band_tiles(t5_table):
    assert T5_MAX_DIST <= LANES
    r = np.arange(LANES)[:, None]
    c = np.arange(LANES)[None, :]
    bmap = np.stack([_t5_bucket(np.maximum(r - c, 0)), _t5_bucket(LANES + r - c)])
    t = t5_table.astype(F32)
    return jnp.transpose(t[bmap] - t[T5_BUCKETS - 1], (3, 0, 1, 2)) * LOG2E


def _with_ones_col(w, heads, d):
    k = w.shape[0]
    dvp = -(-(d + 1) // LANES) * LANES
    wp = jnp.pad(w.reshape(k, heads, d), ((0, 0), (0, 0), (0, dvp - d))).reshape(k, heads * dvp)
    bias = np.zeros((heads, dvp), np.float32)
    bias[:, d] = 1.0
    return wp, jnp.asarray(bias.reshape(-1))


def _hi_lo(a):
    hi = a.astype(BF16)
    lo = (a - hi.astype(F32)).astype(BF16)
    return hi, lo


def _pad_to(a, n, axis):
    pad = [(0, 0)] * a.ndim
    pad[axis] = (0, n - a.shape[axis])
    return jnp.pad(a, pad)


def _rope_tables(s_len):
    pos = jnp.arange(s_len, dtype=F32)
    inv = ROPE_THETA ** (-jnp.arange(0, MLA_ROPE, 2, dtype=F32) / MLA_ROPE)
    ang = pos[:, None] * inv[None, :]
    ang = jnp.concatenate([ang, ang], axis=-1)
    return jnp.cos(ang), jnp.sin(ang)


def _rot_half_matrix():
    half = MLA_ROPE // 2
    r = np.zeros((MLA_ROPE, MLA_ROPE), np.float32)
    for c in range(half):
        r[c + half, c] = -1.0
        r[c, c + half] = 1.0
    return jnp.asarray(r)


def _even_mixer(x2, bsz, s_len, g, w_in, b_forget, w_out, t5_table, tile):
    m, dm = x2.shape
    nk = s_len // tile
    scale = HEAD_DIM ** -0.5 * LOG2E
    wfq, wfk, wfv, wff, wdq, wdk, wdv, wiq, wik, wiw = jnp.split(
        w_in, [int(o) for o in np.cumsum(EVEN_SPLITS)[:-1]], axis=1)
    wfv1, ones_f = _with_ones_col(wfv, FOX_HEADS, HEAD_DIM)
    wdv1, ones_d = _with_ones_col(wdv, DSA_HEADS, HEAD_DIM)
    w_main = jnp.concatenate([wfq * scale, wfk, wfv1, wdq * scale, wdk, wdv1], axis=1).astype(BF16)
    zeros = lambda n: jnp.zeros((n,), F32)
    b_main = jnp.concatenate([zeros(2 * FOX_W), ones_f, zeros(2 * DSA_W), ones_d])
    qoff_f, koff_f, voff_f = 0, FOX_W, 2 * FOX_W
    qoff_d = voff_f + wfv1.shape[1]
    koff_d, voff_d = qoff_d + DSA_W, qoff_d + 2 * DSA_W
    main = _mm(x2, w_main, g=g, bias=b_main, out_dtype=BF16).reshape(bsz, s_len, -1)
    w_aux = jnp.concatenate([_pad_to(jnp.concatenate([wff, wiw], axis=1), LANES, 1),
                             _pad_to(wik, LANES, 1), wiq], axis=1).astype(BF16)
    aux = _mm(x2, w_aux, g=g).reshape(bsz, s_len, -1)
    ff, iw = aux[..., :FOX_HEADS], aux[..., FOX_HEADS:FOX_HEADS + IDX_HEADS]
    ik = aux[..., LANES:LANES + IDX_DIM]
    iq = aux[..., 2 * LANES:2 * LANES + IDX_HEADS * IDX_DIM]

    log_f = jax.nn.log_sigmoid(ff + b_forget.astype(F32))
    log_cum = jnp.cumsum(log_f, axis=1)
    kbias = (-LOG2E * jnp.transpose(log_cum, (0, 2, 1))).reshape(bsz, FOX_HEADS, nk, 1, tile)
    fox = _flash(main, main, main, n_pairs=FOX_HEADS // 2, qoff=qoff_f, koff=koff_f, voff=voff_f,
                 wq=LANES, dv=HEAD_DIM, tile=tile, kbias=kbias)

    q_hi, q_lo = _hi_lo(iq.reshape(bsz, s_len, IDX_HEADS, IDX_DIM))
    k_hi, k_lo = _hi_lo(ik)
    qi3 = jnp.transpose(jnp.concatenate([q_hi, q_hi, q_lo], axis=-1), (0, 2, 1, 3))
    ki3 = jnp.transpose(jnp.concatenate([k_hi, k_lo, k_hi], axis=-1).reshape(bsz, nk, tile, 3 * IDX_DIM),
                        (0, 1, 3, 2))
    mask = _dsa_mask(qi3, ki3, iw, tile, min(DSA_TOPK, s_len // 4))
    dsa = _flash(main, main, main, n_pairs=DSA_HEADS // 2, qoff=qoff_d, koff=koff_d, voff=voff_d,
                 wq=LANES, dv=HEAD_DIM, tile=tile, band=_band_tiles(t5_table), mask=mask)

    mixed = jnp.concatenate([fox, dsa], axis=-1).reshape(m, -1)
    return _mm(mixed, w_out.astype(BF16), res=x2)


def _odd_mixer(x2, bsz, s_len, g, w_in, lq1, lk1, lq2, lk2, subln_g, q_norm_g, w_uq, kv_norm_g, w_ukv,
               w_out, t5_table, lambda_init, tile):
    m, dm = x2.shape
    wcq, wck, wcv, wmq, wmkv, wmkr = jnp.split(
        w_in, [int(o) for o in np.cumsum(ODD_SPLITS)[:-1]], axis=1)
    wcv1, ones_c = _with_ones_col(wcv, DIFF_HEADS, DIFF_VDIM)
    w_main = jnp.concatenate([wcq * (HEAD_DIM ** -0.5 * LOG2E), wck, wcv1], axis=1).astype(BF16)
    b_main = jnp.concatenate([jnp.zeros((2 * DIFF_QK_W,), F32), ones_c])
    main = _mm(x2, w_main, g=g, bias=b_main, out_dtype=BF16).reshape(bsz, s_len, -1)
    lat_w = MLA_Q_RANK + 2 * LANES
    w_aux = _pad_to(jnp.concatenate([wmq, wmkv, wmkr], axis=1), lat_w, 1).astype(BF16)
    aux = _mm(x2, w_aux, g=g)

    lam = (jnp.exp(jnp.sum(lq1.astype(F32) * lk1.astype(F32)))
           - jnp.exp(jnp.sum(lq2.astype(F32) * lk2.astype(F32))) + lambda_init)
    diff = _flash(main, main, main, n_pairs=DIFF_HEADS, qoff=0, koff=DIFF_QK_W, voff=2 * DIFF_QK_W,
                  wq=LANES, dv=DIFF_VDIM, tile=tile, v_shared=True, band=_band_tiles(t5_table),
                  diff=(jnp.full((LANES,), lam, F32), subln_g.astype(F32) * (1.0 - lambda_init)))

    dqk = MLA_NOPE + MLA_ROPE
    cos, sin = _rope_tables(s_len)
    rmat = _rot_half_matrix()
    head_tab = lambda first, rope: jnp.concatenate(
        [jnp.full((s_len, MLA_NOPE), first, F32), rope, jnp.zeros((s_len, LANES - dqk), F32)], axis=1)
    c1, c2 = head_tab(1.0, cos), head_tab(0.0, sin)
    wq3 = w_uq.reshape(MLA_Q_RANK, MLA_HEADS, dqk)
    wq1 = _pad_to(wq3, LANES, 2).reshape(MLA_Q_RANK, -1)
    wq2 = jnp.pad(jnp.einsum('khr,rs->khs', wq3[..., MLA_NOPE:], rmat),
                  ((0, 0), (0, 0), (MLA_NOPE, LANES - dqk))).reshape(MLA_Q_RANK, -1)
    qscale = dqk ** -0.5 * LOG2E
    mla_q = _mm(aux, wq1.astype(BF16), xblk=0, kb=MLA_Q_RANK, g=q_norm_g, w2=wq2.astype(BF16),
                rot=(c1 * qscale, c2 * qscale), out_dtype=BF16).reshape(bsz, s_len, -1)
    wkv3 = w_ukv.reshape(MLA_KV_RANK, MLA_HEADS, MLA_NOPE + MLA_VDIM)
    wk_nope = _pad_to(wkv3[..., :MLA_NOPE], LANES, 2).reshape(MLA_KV_RANK, -1)
    wv1, ones_v = _with_ones_col(wkv3[..., MLA_NOPE:].reshape(MLA_KV_RANK, -1), MLA_HEADS, MLA_VDIM)
    eye_blk = lambda mat: jnp.tile(jnp.pad(mat, ((0, 0), (MLA_NOPE, LANES - dqk))), (1, MLA_HEADS))
    kw = MLA_HEADS * LANES
    top = jnp.concatenate([wk_nope, wv1], axis=1)
    mid1 = jnp.concatenate([eye_blk(jnp.eye(MLA_ROPE, dtype=F32)), jnp.zeros((MLA_ROPE, kw), F32)], axis=1)
    mid2 = jnp.concatenate([eye_blk(rmat), jnp.zeros((MLA_ROPE, kw), F32)], axis=1)
    wk1 = _pad_to(jnp.concatenate([top, mid1], axis=0), 2 * LANES, 0)
    wk2 = _pad_to(jnp.concatenate([jnp.zeros_like(top), mid2], axis=0), 2 * LANES, 0)
    ones_tab = jnp.ones((s_len, LANES), F32)
    g_kv = jnp.concatenate([kv_norm_g.astype(F32), jnp.ones((2 * LANES - MLA_KV_RANK,), F32)])
    mla_kv = _mm(aux, wk1.astype(BF16), xblk=MLA_Q_RANK // (2 * LANES), kb=2 * LANES, g=g_kv,
                 n_norm=MLA_KV_RANK, w2=wk2.astype(BF16),
                 rot=(jnp.concatenate([c1, ones_tab], axis=1), jnp.concatenate([c2, 0.0 * ones_tab], axis=1)),
                 bias=jnp.concatenate([jnp.zeros((kw,), F32), ones_v]),
                 out_dtype=BF16).reshape(bsz, s_len, -1)
    mla = _flash(mla_q, mla_kv, mla_kv, n_pairs=MLA_HEADS // 2, qoff=0, koff=0, voff=kw,
                 wq=2 * LANES, dv=MLA_VDIM, tile=tile)

    mixed = jnp.concatenate([diff, mla], axis=-1).reshape(m, -1)
    return _mm(mixed, w_out.astype(BF16), res=x2)


def kernel(x, norm_mix_g, norm_ffn_g, w_in_even, b_forget, w_out_even, w_in_odd, lambda_q1, lambda_k1, lambda_q2, lambda_k2, diff_subln_g, mla_q_norm_g, w_mla_uq, mla_kv_norm_g, w_mla_ukv, w_out_odd, t5_bias, w_ffn_gate, w_ffn_up, w_ffn_down, final_norm_g):
    bsz, s_len, dm = x.shape
    depth = norm_mix_g.shape[0]
    tile = _pick(s_len, (512, 256, 128))
    x2 = x.astype(F32).reshape(bsz * s_len, dm)
    for layer in range(depth):
        j = layer // 2
        if layer % 2 == 0:
            x2 = _even_mixer(x2, bsz, s_len, norm_mix_g[layer], w_in_even[j], b_forget[j], w_out_even[j],
                             t5_bias, tile)
        else:
            lambda_init = 0.8 - 0.6 * math.exp(-0.3 * layer)
            x2 = _odd_mixer(x2, bsz, s_len, norm_mix_g[layer], w_in_odd[j], lambda_q1[j], lambda_k1[j],
                            lambda_q2[j], lambda_k2[j], diff_subln_g[j], mla_q_norm_g[j], w_mla_uq[j],
                            mla_kv_norm_g[j], w_mla_ukv[j], w_out_odd[j], t5_bias, lambda_init, tile)
        last = layer == depth - 1
        x2 = _ffn(x2, norm_ffn_g[layer], w_ffn_gate[layer].astype(BF16), w_ffn_up[layer].astype(BF16),
                  w_ffn_down[layer].astype(BF16), final_g=final_norm_g if last else None)
    return x2.reshape(bsz, s_len, dm)
```

```python
import functools
import math

import numpy as np
import jax
import jax.numpy as jnp
from jax import lax
from jax.experimental import pallas as pl
from jax.experimental.pallas import tpu as pltpu

F32 = jnp.float32
BF16 = jnp.bfloat16

HEAD_DIM = 64
RMS_EPS = 1e-6
FOX_HEADS = 8
DSA_HEADS = 8
IDX_HEADS = 4
IDX_DIM = 64
DSA_TOPK = 256
DIFF_HEADS = 4
DIFF_VDIM = 2 * HEAD_DIM
MLA_HEADS = 8
MLA_NOPE = 64
MLA_ROPE = 32
MLA_VDIM = 64
MLA_Q_RANK = 256
MLA_KV_RANK = 128
ROPE_THETA = 10000.0
T5_BUCKETS = 32
T5_MAX_DIST = 128

FOX_W = FOX_HEADS * HEAD_DIM
DSA_W = DSA_HEADS * HEAD_DIM
EVEN_SPLITS = [FOX_W, FOX_W, FOX_W, FOX_HEADS, DSA_W, DSA_W, DSA_W,
               IDX_HEADS * IDX_DIM, IDX_DIM, IDX_HEADS]
DIFF_QK_W = DIFF_HEADS * 2 * HEAD_DIM
DIFF_V_W = DIFF_HEADS * DIFF_VDIM
ODD_SPLITS = [DIFF_QK_W, DIFF_QK_W, DIFF_V_W, MLA_Q_RANK, MLA_KV_RANK, MLA_ROPE]

LANES = 128
NEG = -1e30
INT_MIN = -(2 ** 31)
I16_MIN = -(2 ** 15)
VMEM_LIMIT = 48 * 1024 * 1024
FAR_WIDTH = 4
LOG2E = math.log2(math.e)


def _cparams(sem):
    return pltpu.CompilerParams(dimension_semantics=sem, vmem_limit_bytes=VMEM_LIMIT)


def _pick(n, prefs):
    for p in prefs:
        if n % p == 0:
            return p
    return n


def _resident(shape):
    return pl.BlockSpec(shape, lambda i: (0,) * len(shape), pipeline_mode=pl.Buffered(1))


def _mm_kernel(*refs, n_norm, cw, has_g, has_w2, has_rot, has_bias, has_res):
    it = iter(refs)
    x_ref = next(it)
    g_ref = next(it) if has_g else None
    w_ref = next(it)
    w2_ref = next(it) if has_w2 else None
    c1_ref = next(it) if has_rot else None
    c2_ref = next(it) if has_rot else None
    b_ref = next(it) if has_bias else None
    r_ref = next(it) if has_res else None
    o_ref = next(it)

    x = x_ref[...].astype(F32)
    if has_g:
        kb = x.shape[-1]
        if n_norm == kb:
            ms = jnp.mean(x * x, axis=-1, keepdims=True)
            x = x * lax.rsqrt(ms + RMS_EPS) * g_ref[...]
        else:
            normed = lax.broadcasted_iota(jnp.int32, x.shape, 1) < n_norm
            xs = jnp.where(normed, x, 0.0)
            ms = jnp.sum(xs * xs, axis=-1, keepdims=True) * (1.0 / n_norm)
            x = x * jnp.where(normed, lax.rsqrt(ms + RMS_EPS), 1.0) * g_ref[...]
    xn = x.astype(BF16)

    for c in range(o_ref.shape[-1] // cw):
        cs = slice(c * cw, (c + 1) * cw)
        acc = jnp.dot(xn, w_ref[:, cs], preferred_element_type=F32)
        if has_rot:
            r = c1_ref.shape[-1] // LANES
            ts = slice((c % r) * LANES, (c % r + 1) * LANES)
            acc = acc * jnp.concatenate([c1_ref[:, ts]] * (cw // LANES), axis=1)
            acc = acc + jnp.dot(xn, w2_ref[:, cs], preferred_element_type=F32) * jnp.concatenate(
                [c2_ref[:, ts]] * (cw // LANES), axis=1)
        if has_bias:
            acc = acc + b_ref[:, cs]
        if has_res:
            acc = acc + r_ref[:, cs]
        o_ref[:, cs] = acc.astype(o_ref.dtype)


def _mm(x, w, *, xblk=0, kb=None, g=None, n_norm=None, w2=None, rot=None, bias=None, res=None,
        out_dtype=F32):
    m = x.shape[0]
    kb = x.shape[1] if kb is None else kb
    n = w.shape[1]
    tm = _pick(m, (512, 256, 128))
    r = 1 if rot is None else rot[0].shape[1] // LANES
    cw = n // r if rot is not None else _pick(n, (1024, 768, 512, 384, 256, 128))
    args = [x]
    specs = [pl.BlockSpec((tm, kb), lambda i: (i, xblk))]
    if g is not None:
        n_norm = kb if n_norm is None else n_norm
        args.append(g.reshape(1, kb).astype(F32))
        specs.append(_resident((1, kb)))
    args.append(w)
    specs.append(_resident((kb, n)))
    if w2 is not None:
        args.append(w2)
        specs.append(_resident((kb, n)))
    if rot is not None:
        s_len = rot[0].shape[0]
        assert s_len % tm == 0
        for c in rot:
            args.append(c)
            specs.append(pl.BlockSpec((tm, r * LANES), lambda i: (i % (s_len // tm), 0)))
    if bias is not None:
        args.append(bias.reshape(1, n).astype(F32))
        specs.append(_resident((1, n)))
    if res is not None:
        args.append(res)
        specs.append(pl.BlockSpec((tm, n), lambda i: (i, 0)))
    return pl.pallas_call(
        functools.partial(_mm_kernel, n_norm=n_norm, cw=cw, has_g=g is not None, has_w2=w2 is not None,
                          has_rot=rot is not None, has_bias=bias is not None, has_res=res is not None),
        grid=(m // tm,),
        in_specs=specs,
        out_specs=pl.BlockSpec((tm, n), lambda i: (i, 0)),
        out_shape=jax.ShapeDtypeStruct((m, n), out_dtype),
        compiler_params=_cparams(("parallel",)),
        name="mm",
    )(*args)


def _ffn_kernel(*refs, cf, has_final):
    it = iter(refs)
    x_ref, g_ref, wg_ref, wu_ref, wd_ref = (next(it) for _ in range(5))
    fg_ref = next(it) if has_final else None
    o_ref = next(it)

    x = x_ref[...]
    ms = jnp.mean(x * x, axis=-1, keepdims=True)
    h = (x * lax.rsqrt(ms + RMS_EPS) * g_ref[...]).astype(BF16)
    y = x
    for c in range(wg_ref.shape[-1] // cf):
        cs = slice(c * cf, (c + 1) * cf)
        gate = jnp.dot(h, wg_ref[:, cs], preferred_element_type=F32)
        up = jnp.dot(h, wu_ref[:, cs], preferred_element_type=F32)
        act = gate * jax.nn.sigmoid(gate) * up
        y = y + jnp.dot(act.astype(BF16), wd_ref[cs, :], preferred_element_type=F32)
    if has_final:
        ms = jnp.mean(y * y, axis=-1, keepdims=True)
        y = y * lax.rsqrt(ms + RMS_EPS) * fg_ref[...]
    o_ref[...] = y


def _ffn(x, g, wg, wu, wd, final_g=None):
    m, d = x.shape
    hid = wg.shape[1]
    tm = _pick(m, (512, 256, 128))
    args = [x, g.reshape(1, d).astype(F32), wg, wu, wd]
    specs = [pl.BlockSpec((tm, d), lambda i: (i, 0)), _resident((1, d)),
             _resident((d, hid)), _resident((d, hid)), _resident((hid, d))]
    if final_g is not None:
        args.append(final_g.reshape(1, d).astype(F32))
        specs.append(_resident((1, d)))
    return pl.pallas_call(
        functools.partial(_ffn_kernel, cf=_pick(hid, (256, 128)), has_final=final_g is not None),
        grid=(m // tm,),
        in_specs=specs,
        out_specs=pl.BlockSpec((tm, d), lambda i: (i, 0)),
        out_shape=jax.ShapeDtypeStruct((m, d), F32),
        compiler_params=_cparams(("parallel",)),
        name="ffn",
    )(*args)


def _dsa_mask_kernel(qi_ref, ki_ref, w_ref, o_ref, keys_ref, hi_ref, lo_ref, cand_ref, *,
                     tq, tk, nk, ksel, idx_scale):
    i = pl.program_id(1)
    groups = tk // LANES
    n_full = (i * tq) // tk
    off = i * tq - n_full * tk
    w = w_ref[0]
    kf = float(ksel)

    def score_chunk(j, diag):
        kc = ki_ref[0, j]
        sc = jnp.zeros((tq, tk), F32)
        for h in range(IDX_HEADS):
            d = jnp.dot(qi_ref[0, h], kc, preferred_element_type=F32)
            sc = sc + jnp.maximum(d, 0.0) * w[:, h:h + 1]
        sc = sc * idx_scale
        bits = lax.bitcast_convert_type(sc, jnp.int32)
        key = jnp.where(bits < 0, INT_MIN - bits, bits)
        if diag:
            row = lax.broadcasted_iota(jnp.int32, (tq, tk), 0)
            col = lax.broadcasted_iota(jnp.int32, (tq, tk), 1)
            key = jnp.where(col <= row + off, key, INT_MIN)
        keys_ref[j] = key
        hi_ref[j] = lax.shift_right_arithmetic(key, 16).astype(jnp.int16)
        lo_ref[j] = ((key & 0xFFFF) + I16_MIN).astype(jnp.int16)

    def _plain(j, c):
        score_chunk(j, False)
        return c

    lax.fori_loop(0, n_full, _plain, 0)
    score_chunk(n_full, True)

    ones_l = jnp.ones((LANES, LANES), BF16)

    def count(src_ref):
        def body(j, acc):
            kc = src_ref[j]
            cand = cand_ref[...]
            for g in range(groups):
                acc = acc + jnp.where(kc[:, g * LANES:(g + 1) * LANES] >= cand, one_b, zero_b)
            return acc
        acc = lax.fori_loop(0, n_full + 1, body, jnp.zeros((tq, LANES), BF16))
        return jnp.dot(acc, ones_l, preferred_element_type=F32)

    def kth_largest16(src_ref, kth):
        cand_ref[...] = jnp.zeros((tq, LANES), jnp.int16)
        cnt = count(src_ref)
        ok = cnt >= kth
        state = (jnp.where(ok, 0, I16_MIN).astype(jnp.int32), jnp.where(ok, 0.0, cnt))

        def bit_body(b, state):
            prefix, above = state
            cand = prefix + jnp.left_shift(jnp.int32(1), 14 - b)
            cand_ref[...] = cand.astype(jnp.int16)
            cnt = count(src_ref)
            ok = cnt >= kth
            return jnp.where(ok, cand, prefix), jnp.where(ok, above, cnt)

        return lax.fori_loop(0, 15, bit_body, state)

    one_b, zero_b = jnp.ones((), BF16), jnp.zeros((), BF16)
    h_top, above_h = kth_largest16(hi_ref, kf)
    k_low = kf - above_h
    h16_t = jnp.concatenate([h_top.astype(jnp.int16)] * groups, axis=1)

    def _restrict(j, c):
        lo_ref[j] = jnp.where(hi_ref[j] == h16_t, lo_ref[j], jnp.int16(I16_MIN))
        return c

    lax.fori_loop(0, n_full + 1, _restrict, 0)
    l_top, above_l = kth_largest16(lo_ref, k_low)
    need = k_low - above_l
    tau = h_top * 65536 + (l_top - I16_MIN)

    tri = (lax.broadcasted_iota(jnp.int32, (tk, tk), 0)
           <= lax.broadcasted_iota(jnp.int32, (tk, tk), 1)).astype(BF16)
    ones_t = jnp.ones((tk, LANES), BF16)
    tau_t = jnp.concatenate([tau] * groups, axis=1)
    need_t = jnp.concatenate([need] * groups, axis=1)

    def emit(j, carry):
        kc = keys_ref[j]
        eq = kc == tau_t
        eqf = jnp.where(eq, 1.0, 0.0).astype(BF16)
        rank = jnp.dot(eqf, tri, preferred_element_type=F32) + jnp.concatenate([carry] * groups, axis=1)
        sel = (kc > tau_t) | (eq & (rank <= need_t) & (kc > INT_MIN))
        o_ref[0, j] = jnp.where(sel, 0.0, NEG).astype(o_ref.dtype)
        return carry + jnp.dot(eqf, ones_t, preferred_element_type=F32)

    lax.fori_loop(0, n_full + 1, emit, jnp.zeros((tq, LANES), F32))

    def _fill(j, c):
        o_ref[0, j] = jnp.full((tq, tk), NEG, o_ref.dtype)
        return c

    lax.fori_loop(n_full + 1, nk, _fill, 0)


def _dsa_mask(qi3, ki3, w, tk, ksel):
    bsz, hi, s_len, d3 = qi3.shape
    nk = s_len // tk
    tq = _pick(tk, (256, 128))
    assert s_len // LANES <= 256, "per-lane counts must stay exact in bf16"
    idx_scale = (IDX_DIM ** -0.5) * (IDX_HEADS ** -0.5)
    return pl.pallas_call(
        functools.partial(_dsa_mask_kernel, tq=tq, tk=tk, nk=nk, ksel=ksel, idx_scale=idx_scale),
        grid=(bsz, s_len // tq),
        in_specs=[pl.BlockSpec((1, hi, tq, d3), lambda b, i: (b, 0, i, 0)),
                  pl.BlockSpec((1, nk, d3, tk), lambda b, i: (b, 0, 0, 0)),
                  pl.BlockSpec((1, tq, hi), lambda b, i: (b, i, 0))],
        out_specs=pl.BlockSpec((1, nk, tq, tk), lambda b, i: (b, 0, i, 0)),
        out_shape=jax.ShapeDtypeStruct((bsz, nk, s_len, tk), BF16),
        scratch_shapes=[pltpu.VMEM((nk, tq, tk), jnp.int32), pltpu.VMEM((nk, tq, tk), jnp.int16),
                        pltpu.VMEM((nk, tq, tk), jnp.int16), pltpu.VMEM((tq, LANES), jnp.int16)],
        compiler_params=_cparams(("parallel", "arbitrary")),
        name="dsa_mask",
    )(qi3, ki3, w)


def _finish_tail(s, band, causal):
    nr, nc = s.shape[0] // LANES, s.shape[1] // LANES
    keep = (lax.broadcasted_iota(jnp.int32, (LANES, LANES), 1)
            <= lax.broadcasted_iota(jnp.int32, (LANES, LANES), 0))
    rows = []
    for a in range(nr):
        blocks = []
        for b in range(nc):
            d = (nc - nr) + a - b
            blk = s[a * LANES:(a + 1) * LANES, b * LANES:(b + 1) * LANES]
            if band is not None and d in (0, 1):
                blk = blk + band[d]
            if causal and d == 0:
                blk = jnp.where(keep, blk, NEG)
            elif causal and d < 0:
                blk = jnp.full_like(blk, NEG)
            blocks.append(blk)
        rows.append(jnp.concatenate(blocks, axis=1))
    return jnp.concatenate(rows, axis=0)


def _flash_kernel(*refs, tile, wq, dv, v_shared, q_axis, has_kbias, has_band, has_mask, diff):
    it = iter(refs)
    q_ref, k_ref, v_ref = next(it), next(it), next(it)
    kb_ref = next(it) if has_kbias else None
    t_ref = next(it) if has_band else None
    mk_ref = next(it) if has_mask else None
    lam_ref = next(it) if diff else None
    sg_ref = next(it) if diff else None
    o_ref, m_ref, acc_ref = next(it), next(it), next(it)

    i = pl.program_id(q_axis)
    dvp = acc_ref.shape[-1]
    a_groups = dvp // LANES
    qblk = q_ref[0]
    if wq == LANES:
        lane = lax.broadcasted_iota(jnp.int32, qblk.shape, 1)
        q32 = qblk.astype(F32)
        qs = [jnp.where(lane < LANES // 2, q32, 0.0).astype(BF16),
              jnp.where(lane >= LANES // 2, q32, 0.0).astype(BF16)]
    else:
        qs = [qblk[:, :LANES], qblk[:, LANES:]]
    m_ref[...] = jnp.full_like(m_ref, NEG)
    acc_ref[...] = jnp.zeros_like(acc_ref)

    def step(j, width=1, tail=False):
        cols = width * tile
        off = pl.multiple_of(j * tile, tile)
        kc = k_ref[0, pl.ds(off, cols), :]
        vc = v_ref[0, pl.ds(off, cols), :]
        cat = lambda f: f(j) if width == 1 else jnp.concatenate([f(j + t) for t in range(width)], axis=1)
        mk = cat(lambda t: mk_ref[0, t]).astype(F32) if has_mask else None
        for a in range(2):
            ka = kc if wq == LANES else kc[:, a * LANES:(a + 1) * LANES]
            s = lax.dot_general(qs[a], ka, (((1,), (1,)), ((), ())), preferred_element_type=F32)
            if has_kbias:
                s = s + cat(lambda t: kb_ref[0, a, t])
            if tail:
                s = _finish_tail(s, (t_ref[a, 0], t_ref[a, 1]) if has_band else None, not has_mask)
            if has_mask:
                s = s + mk
            m_old = m_ref[a]
            m_new = jnp.maximum(m_old, jnp.max(s, axis=-1, keepdims=True))
            p = jnp.exp2(s - jnp.concatenate([m_new] * (cols // LANES), axis=1))
            alpha = jnp.exp2(m_old - m_new)
            va = vc if v_shared else vc[:, a * LANES:(a + 1) * LANES]
            acc_ref[a] = acc_ref[a] * jnp.concatenate([alpha] * a_groups, axis=1) + jnp.dot(
                p.astype(BF16), va, preferred_element_type=F32)
            m_ref[a] = m_new

    n_far = jnp.maximum(i - 1, 0)

    def _far(t, c):
        step(t * FAR_WIDTH, width=FAR_WIDTH)
        return c

    lax.fori_loop(0, n_far // FAR_WIDTH, _far, 0)
    w = FAR_WIDTH // 2
    while w >= 1:
        @pl.when(n_far % (2 * w) >= w)
        def _(w=w):
            step((n_far // (2 * w)) * (2 * w), width=w)
        w //= 2

    @pl.when(i >= 1)
    def _():
        step(i - 1, width=2, tail=True)

    @pl.when(i == 0)
    def _():
        step(0, tail=True)

    outs = []
    for a in range(2):
        acc = acc_ref[a]
        outs.append(acc[:, :dv] / acc[:, dv:dv + 1])
    if diff:
        d = outs[0] - lam_ref[...] * outs[1]
        ms = jnp.mean(d * d, axis=-1, keepdims=True)
        o_ref[0] = (d * lax.rsqrt(ms + RMS_EPS) * sg_ref[...]).astype(o_ref.dtype)
    else:
        o_ref[0] = jnp.concatenate(outs, axis=1).astype(o_ref.dtype)


def _flash(qa, ka, va, *, n_pairs, qoff, koff, voff, wq, dv, tile, v_shared=False, kbias=None,
           band=None, mask=None, diff=None):
    bsz, s_len, _ = qa.shape
    nk = s_len // tile
    dvp = 2 * LANES if v_shared else LANES
    assert qoff % wq == 0 and koff % wq == 0 and voff % (2 * LANES) == 0
    qb, kb_, vb = qoff // wq, koff // wq, voff // (2 * LANES)
    if mask is None:
        grid, q_axis = (bsz, n_pairs, nk), 2
        ix = lambda f: (lambda b, h, i: f(b, h, i))
    else:
        grid, q_axis = (bsz, nk, n_pairs), 1
        ix = lambda f: (lambda b, i, h: f(b, h, i))
    args = [qa, ka, va]
    specs = [pl.BlockSpec((1, tile, wq), ix(lambda b, h, i: (b, i, qb + h))),
             pl.BlockSpec((1, s_len, wq), ix(lambda b, h, i: (b, 0, kb_ + h))),
             pl.BlockSpec((1, s_len, 2 * LANES), ix(lambda b, h, i: (b, 0, vb + h)))]
    if kbias is not None:
        args.append(kbias)
        specs.append(pl.BlockSpec((1, 2, nk, 1, tile), ix(lambda b, h, i: (b, h, 0, 0, 0))))
    if band is not None:
        args.append(band)
        specs.append(pl.BlockSpec((2, 2, LANES, LANES), ix(lambda b, h, i: (h, 0, 0, 0))))
    if mask is not None:
        args.append(mask)
        specs.append(pl.BlockSpec((1, nk, tile, tile), ix(lambda b, h, i: (b, 0, i, 0))))
    if diff is not None:
        for t in diff:
            args.append(t.reshape(1, LANES).astype(F32))
            specs.append(pl.BlockSpec((1, LANES), ix(lambda b, h, i: (0, 0))))
    return pl.pallas_call(
        functools.partial(_flash_kernel, tile=tile, wq=wq, dv=dv, v_shared=v_shared, q_axis=q_axis,
                          has_kbias=kbias is not None, has_band=band is not None,
                          has_mask=mask is not None, diff=diff is not None),
        grid=grid,
        in_specs=specs,
        out_specs=pl.BlockSpec((1, tile, LANES), ix(lambda b, h, i: (b, i, h))),
        out_shape=jax.ShapeDtypeStruct((bsz, s_len, n_pairs * LANES), BF16),
        scratch_shapes=[pltpu.VMEM((2, tile, LANES), F32), pltpu.VMEM((2, tile, dvp), F32)],
        compiler_params=_cparams(("parallel", "parallel", "arbitrary")),
        name="flash",
    )(*args)


def _t5_bucket(dist):
    exact = T5_BUCKETS // 2
    d = np.maximum(dist, 1).astype(np.float32)
    log_b = exact + (np.log(d / np.float32(exact)) / np.float32(math.log(T5_MAX_DIST / exact))
                     * np.float32(T5_BUCKETS - exact)).astype(np.int32)
    log_b = np.minimum(log_b, T5_BUCKETS - 1)
    return np.where(dist < exact, dist, log_b).astype(np.int32)


def _band_tiles(t5_table):
    assert T5_MAX_DIST <= LANES
    r = np.arange(LANES)[:, None]
    c = np.arange(LANES)[None, :]
    bmap = np.stack([_t5_bucket(np.maximum(r - c, 0)), _t5_bucket(LANES + r - c)])
    t = t5_table.astype(F32)
    return jnp.transpose(t[bmap] - t[T5_BUCKETS - 1], (3, 0, 1, 2)) * LOG2E


def _with_ones_col(w, heads, d):
    k = w.shape[0]
    dvp = -(-(d + 1) // LANES) * LANES
    wp = jnp.pad(w.reshape(k, heads, d), ((0, 0), (0, 0), (0, dvp - d))).reshape(k, heads * dvp)
    bias = np.zeros((heads, dvp), np.float32)
    bias[:, d] = 1.0
    return wp, jnp.asarray(bias.reshape(-1))


def _hi_lo(a):
    hi = a.astype(BF16)
    lo = (a - hi.astype(F32)).astype(BF16)
    return hi, lo


def _pad_to(a, n, axis):
    pad = [(0, 0)] * a.ndim
    pad[axis] = (0, n - a.shape[axis])
    return jnp.pad(a, pad)


def _rope_tables(s_len):
    pos = jnp.arange(s_len, dtype=F32)
    inv = ROPE_THETA ** (-jnp.arange(0, MLA_ROPE, 2, dtype=F32) / MLA_ROPE)
    ang = pos[:, None] * inv[None, :]
    ang = jnp.concatenate([ang, ang], axis=-1)
    return jnp.cos(ang), jnp.sin(ang)


def _rot_half_matrix():
    half = MLA_ROPE // 2
    r = np.zeros((MLA_ROPE, MLA_ROPE), np.float32)
    for c in range(half):
        r[c + half, c] = -1.0
        r[c, c + half] = 1.0
    return jnp.asarray(r)


def _even_mixer(x2, bsz, s_len, g, w_in, b_forget, w_out, t5_table, tile):
    m, dm = x2.shape
    nk = s_len // tile
    scale = HEAD_DIM ** -0.5 * LOG2E
    wfq, wfk, wfv, wff, wdq, wdk, wdv, wiq, wik, wiw = jnp.split(
        w_in, [int(o) for o in np.cumsum(EVEN_SPLITS)[:-1]], axis=1)
    wfv1, ones_f = _with_ones_col(wfv, FOX_HEADS, HEAD_DIM)
    wdv1, ones_d = _with_ones_col(wdv, DSA_HEADS, HEAD_DIM)
    w_main = jnp.concatenate([wfq * scale, wfk, wfv1, wdq * scale, wdk, wdv1], axis=1).astype(BF16)
    zeros = lambda n: jnp.zeros((n,), F32)
    b_main = jnp.concatenate([zeros(2 * FOX_W), ones_f, zeros(2 * DSA_W), ones_d])
    qoff_f, koff_f, voff_f = 0, FOX_W, 2 * FOX_W
    qoff_d = voff_f + wfv1.shape[1]
    koff_d, voff_d = qoff_d + DSA_W, qoff_d + 2 * DSA_W
    main = _mm(x2, w_main, g=g, bias=b_main, out_dtype=BF16).reshape(bsz, s_len, -1)
    w_aux = jnp.concatenate([_pad_to(jnp.concatenate([wff, wiw], axis=1), LANES, 1),
                             _pad_to(wik, LANES, 1), wiq], axis=1).astype(BF16)
    aux = _mm(x2, w_aux, g=g).reshape(bsz, s_len, -1)
    ff, iw = aux[..., :FOX_HEADS], aux[..., FOX_HEADS:FOX_HEADS + IDX_HEADS]
    ik = aux[..., LANES:LANES + IDX_DIM]
    iq = aux[..., 2 * LANES:2 * LANES + IDX_HEADS * IDX_DIM]

    log_f = jax.nn.log_sigmoid(ff + b_forget.astype(F32))
    log_cum = jnp.cumsum(log_f, axis=1)
    kbias = (-LOG2E * jnp.transpose(log_cum, (0, 2, 1))).reshape(bsz, FOX_HEADS, nk, 1, tile)
    fox = _flash(main, main, main, n_pairs=FOX_HEADS // 2, qoff=qoff_f, koff=koff_f, voff=voff_f,
                 wq=LANES, dv=HEAD_DIM, tile=tile, kbias=kbias)

    q_hi, q_lo = _hi_lo(iq.reshape(bsz, s_len, IDX_HEADS, IDX_DIM))
    k_hi, k_lo = _hi_lo(ik)
    qi3 = jnp.transpose(jnp.concatenate([q_hi, q_hi, q_lo], axis=-1), (0, 2, 1, 3))
    ki3 = jnp.transpose(jnp.concatenate([k_hi, k_lo, k_hi], axis=-1).reshape(bsz, nk, tile, 3 * IDX_DIM),
                        (0, 1, 3, 2))
    mask = _dsa_mask(qi3, ki3, iw, tile, min(DSA_TOPK, s_len // 4))
    dsa = _flash(main, main, main, n_pairs=DSA_HEADS // 2, qoff=qoff_d, koff=koff_d, voff=voff_d,
                 wq=LANES, dv=HEAD_DIM, tile=tile, band=_band_tiles(t5_table), mask=mask)

    mixed = jnp.concatenate([fox, dsa], axis=-1).reshape(m, -1)
    return _mm(mixed, w_out.astype(BF16), res=x2)


def _odd_mixer(x2, bsz, s_len, g, w_in, lq1, lk1, lq2, lk2, subln_g, q_norm_g, w_uq, kv_norm_g, w_ukv,
               w_out, t5_table, lambda_init, tile):
    m, dm = x2.shape
    wcq, wck, wcv, wmq, wmkv, wmkr = jnp.split(
        w_in, [int(o) for o in np.cumsum(ODD_SPLITS)[:-1]], axis=1)
    wcv1, ones_c = _with_ones_col(wcv, DIFF_HEADS, DIFF_VDIM)
    w_main = jnp.concatenate([wcq * (HEAD_DIM ** -0.5 * LOG2E), wck, wcv1], axis=1).astype(BF16)
    b_main = jnp.concatenate([jnp.zeros((2 * DIFF_QK_W,), F32), ones_c])
    main = _mm(x2, w_main, g=g, bias=b_main, out_dtype=BF16).reshape(bsz, s_len, -1)
    lat_w = MLA_Q_RANK + 2 * LANES
    w_aux = _pad_to(jnp.concatenate([wmq, wmkv, wmkr], axis=1), lat_w, 1).astype(BF16)
    aux = _mm(x2, w_aux, g=g)

    lam = (jnp.exp(jnp.sum(lq1.astype(F32) * lk1.astype(F32)))
           - jnp.exp(jnp.sum(lq2.astype(F32) * lk2.astype(F32))) + lambda_init)
    diff = _flash(main, main, main, n_pairs=DIFF_HEADS, qoff=0, koff=DIFF_QK_W, voff=2 * DIFF_QK_W,
                  wq=LANES, dv=DIFF_VDIM, tile=tile, v_shared=True, band=_band_tiles(t5_table),
                  diff=(jnp.full((LANES,), lam, F32), subln_g.astype(F32) * (1.0 - lambda_init)))

    dqk = MLA_NOPE + MLA_ROPE
    cos, sin = _rope_tables(s_len)
    rmat = _rot_half_matrix()
    head_tab = lambda first, rope: jnp.concatenate(
        [jnp.full((s_len, MLA_NOPE), first, F32), rope, jnp.zeros((s_len, LANES - dqk), F32)], axis=1)
    c1, c2 = head_tab(1.0, cos), head_tab(0.0, sin)
    wq3 = w_uq.reshape(MLA_Q_RANK, MLA_HEADS, dqk)
    wq1 = _pad_to(wq3, LANES, 2).reshape(MLA_Q_RANK, -1)
    wq2 = jnp.pad(jnp.einsum('khr,rs->khs', wq3[..., MLA_NOPE:], rmat),
                  ((0, 0), (0, 0), (MLA_NOPE, LANES - dqk))).reshape(MLA_Q_RANK, -1)
    qscale = dqk ** -0.5 * LOG2E
    mla_q = _mm(aux, wq1.astype(BF16), xblk=0, kb=MLA_Q_RANK, g=q_norm_g, w2=wq2.astype(BF16),
                rot=(c1 * qscale, c2 * qscale), out_dtype=BF16).reshape(bsz, s_len, -1)
    wkv3 = w_ukv.reshape(MLA_KV_RANK, MLA_HEADS, MLA_NOPE + MLA_VDIM)
    wk_nope = _pad_to(wkv3[..., :MLA_NOPE], LANES, 2).reshape(MLA_KV_RANK, -1)
    wv1, ones_v = _with_ones_col(wkv3[..., MLA_NOPE:].reshape(MLA_KV_RANK, -1), MLA_HEADS, MLA_VDIM)
    eye_blk = lambda mat: jnp.tile(jnp.pad(mat, ((0, 0), (MLA_NOPE, LANES - dqk))), (1, MLA_HEADS))
    kw = MLA_HEADS * LANES
    top = jnp.concatenate([wk_nope, wv1], axis=1)
    mid1 = jnp.concatenate([eye_blk(jnp.eye(MLA_ROPE, dtype=F32)), jnp.zeros((MLA_ROPE, kw), F32)], axis=1)
    mid2 = jnp.concatenate([eye_blk(rmat), jnp.zeros((MLA_ROPE, kw), F32)], axis=1)
    wk1 = _pad_to(jnp.concatenate([top, mid1], axis=0), 2 * LANES, 0)
    wk2 = _pad_to(jnp.concatenate([jnp.zeros_like(top), mid2], axis=0), 2 * LANES, 0)
    ones_tab = jnp.ones((s_len, LANES), F32)
    g_kv = jnp.concatenate([kv_norm_g.astype(F32), jnp.ones((2 * LANES - MLA_KV_RANK,), F32)])
    mla_kv = _mm(aux, wk1.astype(BF16), xblk=MLA_Q_RANK // (2 * LANES), kb=2 * LANES, g=g_kv,
                 n_norm=MLA_KV_RANK, w2=wk2.astype(BF16),
                 rot=(jnp.concatenate([c1, ones_tab], axis=1), jnp.concatenate([c2, 0.0 * ones_tab], axis=1)),
                 bias=jnp.concatenate([jnp.zeros((kw,), F32), ones_v]),
                 out_dtype=BF16).reshape(bsz, s_len, -1)
    mla = _flash(mla_q, mla_kv, mla_kv, n_pairs=MLA_HEADS // 2, qoff=0, koff=0, voff=kw,
                 wq=2 * LANES, dv=MLA_VDIM, tile=tile)

    mixed = jnp.concatenate([diff, mla], axis=-1).reshape(m, -1)
    return _mm(mixed, w_out.astype(BF16), res=x2)


def kernel(x, norm_mix_g, norm_ffn_g, w_in_even, b_forget, w_out_even, w_in_odd, lambda_q1, lambda_k1, lambda_q2, lambda_k2, diff_subln_g, mla_q_norm_g, w_mla_uq, mla_kv_norm_g, w_mla_ukv, w_out_odd, t5_bias, w_ffn_gate, w_ffn_up, w_ffn_down, final_norm_g):
    bsz, s_len, dm = x.shape
    depth = norm_mix_g.shape[0]
    tile = _pick(s_len, (512, 256, 128))
    x2 = x.astype(F32).reshape(bsz * s_len, dm)
    for layer in range(depth):
        j = layer // 2
        if layer % 2 == 0:
            x2 = _even_mixer(x2, bsz, s_len, norm_mix_g[layer], w_in_even[j], b_forget[j], w_out_even[j],
                             t5_bias, tile)
        else:
            lambda_init = 0.8 - 0.6 * math.exp(-0.3 * layer)
            x2 = _odd_mixer(x2, bsz, s_len, norm_mix_g[layer], w_in_odd[j], lambda_q1[j], lambda_k1[j],
                            lambda_q2[j], lambda_k2[j], diff_subln_g[j], mla_q_norm_g[j], w_mla_uq[j],
                            mla_kv_norm_g[j], w_mla_ukv[j], w_out_odd[j], t5_bias, lambda_init, tile)
        last = layer == depth - 1
        x2 = _ffn(x2, norm_ffn_g[layer], w_ffn_gate[layer].astype(BF16), w_ffn_up[layer].astype(BF16),
                  w_ffn_down[layer].astype(BF16), final_g=final_norm_g if last else None)
    return x2.reshape(bsz, s_len, dm)
```

```python
import functools
import math

import numpy as np
import jax
import jax.numpy as jnp
from jax import lax
from jax.experimental import pallas as pl
from jax.experimental.pallas import tpu as pltpu

F32 = jnp.float32
BF16 = jnp.bfloat16

HEAD_DIM = 64
RMS_EPS = 1e-6
FOX_HEADS = 8
DSA_HEADS = 8
IDX_HEADS = 4
IDX_DIM = 64
DSA_TOPK = 256
DIFF_HEADS = 4
DIFF_VDIM = 2 * HEAD_DIM
MLA_HEADS = 8
MLA_NOPE = 64
MLA_ROPE = 32
MLA_VDIM = 64
MLA_Q_RANK = 256
MLA_KV_RANK = 128
ROPE_THETA = 10000.0
T5_BUCKETS = 32
T5_MAX_DIST = 128

FOX_W = FOX_HEADS * HEAD_DIM
DSA_W = DSA_HEADS * HEAD_DIM
EVEN_SPLITS = [FOX_W, FOX_W, FOX_W, FOX_HEADS, DSA_W, DSA_W, DSA_W,
               IDX_HEADS * IDX_DIM, IDX_DIM, IDX_HEADS]
DIFF_QK_W = DIFF_HEADS * 2 * HEAD_DIM
DIFF_V_W = DIFF_HEADS * DIFF_VDIM
ODD_SPLITS = [DIFF_QK_W, DIFF_QK_W, DIFF_V_W, MLA_Q_RANK, MLA_KV_RANK, MLA_ROPE]

LANES = 128
NEG = -1e30
INT_MIN = -(2 ** 31)
VMEM_LIMIT = 48 * 1024 * 1024
FAR_WIDTH = 4
LOG2E = math.log2(math.e)


def _cparams(sem):
    return pltpu.CompilerParams(dimension_semantics=sem, vmem_limit_bytes=VMEM_LIMIT)


def _pick(n, prefs):
    for p in prefs:
        if n % p == 0:
            return p
    return n


def _resident(shape):
    return pl.BlockSpec(shape, lambda i: (0,) * len(shape), pipeline_mode=pl.Buffered(1))


def _mm_kernel(*refs, n_norm, cw, has_g, has_w2, has_rot, has_bias, has_res):
    it = iter(refs)
    x_ref = next(it)
    g_ref = next(it) if has_g else None
    w_ref = next(it)
    w2_ref = next(it) if has_w2 else None
    c1_ref = next(it) if has_rot else None
    c2_ref = next(it) if has_rot else None
    b_ref = next(it) if has_bias else None
    r_ref = next(it) if has_res else None
    o_ref = next(it)

    x = x_ref[...].astype(F32)
    if has_g:
        kb = x.shape[-1]
        if n_norm == kb:
            ms = jnp.mean(x * x, axis=-1, keepdims=True)
            x = x * lax.rsqrt(ms + RMS_EPS) * g_ref[...]
        else:
            normed = lax.broadcasted_iota(jnp.int32, x.shape, 1) < n_norm
            xs = jnp.where(normed, x, 0.0)
            ms = jnp.sum(xs * xs, axis=-1, keepdims=True) * (1.0 / n_norm)
            x = x * jnp.where(normed, lax.rsqrt(ms + RMS_EPS), 1.0) * g_ref[...]
    xn = x.astype(BF16)

    for c in range(o_ref.shape[-1] // cw):
        cs = slice(c * cw, (c + 1) * cw)
        acc = jnp.dot(xn, w_ref[:, cs], preferred_element_type=F32)
        if has_rot:
            r = c1_ref.shape[-1] // LANES
            ts = slice((c % r) * LANES, (c % r + 1) * LANES)
            acc = acc * jnp.concatenate([c1_ref[:, ts]] * (cw // LANES), axis=1)
            acc = acc + jnp.dot(xn, w2_ref[:, cs], preferred_element_type=F32) * jnp.concatenate(
                [c2_ref[:, ts]] * (cw // LANES), axis=1)
        if has_bias:
            acc = acc + b_ref[:, cs]
        if has_res:
            acc = acc + r_ref[:, cs]
        o_ref[:, cs] = acc.astype(o_ref.dtype)


def _mm(x, w, *, xblk=0, kb=None, g=None, n_norm=None, w2=None, rot=None, bias=None, res=None,
        out_dtype=F32):
    m = x.shape[0]
    kb = x.shape[1] if kb is None else kb
    n = w.shape[1]
    tm = _pick(m, (512, 256, 128))
    r = 1 if rot is None else rot[0].shape[1] // LANES
    cw = n // r if rot is not None else _pick(n, (1024, 768, 512, 384, 256, 128))
    args = [x]
    specs = [pl.BlockSpec((tm, kb), lambda i: (i, xblk))]
    if g is not None:
        n_norm = kb if n_norm is None else n_norm
        args.append(g.reshape(1, kb).astype(F32))
        specs.append(_resident((1, kb)))
    args.append(w)
    specs.append(_resident((kb, n)))
    if w2 is not None:
        args.append(w2)
        specs.append(_resident((kb, n)))
    if rot is not None:
        s_len = rot[0].shape[0]
        assert s_len % tm == 0
        for c in rot:
            args.append(c)
            specs.append(pl.BlockSpec((tm, r * LANES), lambda i: (i % (s_len // tm), 0)))
    if bias is not None:
        args.append(bias.reshape(1, n).astype(F32))
        specs.append(_resident((1, n)))
    if res is not None:
        args.append(res)
        specs.append(pl.BlockSpec((tm, n), lambda i: (i, 0)))
    return pl.pallas_call(
        functools.partial(_mm_kernel, n_norm=n_norm, cw=cw, has_g=g is not None, has_w2=w2 is not None,
                          has_rot=rot is not None, has_bias=bias is not None, has_res=res is not None),
        grid=(m // tm,),
        in_specs=specs,
        out_specs=pl.BlockSpec((tm, n), lambda i: (i, 0)),
        out_shape=jax.ShapeDtypeStruct((m, n), out_dtype),
        compiler_params=_cparams(("parallel",)),
        name="mm",
    )(*args)


def _ffn_kernel(*refs, cf, has_final):
    it = iter(refs)
    x_ref, g_ref, wg_ref, wu_ref, wd_ref = (next(it) for _ in range(5))
    fg_ref = next(it) if has_final else None
    o_ref = next(it)

    x = x_ref[...]
    ms = jnp.mean(x * x, axis=-1, keepdims=True)
    h = (x * lax.rsqrt(ms + RMS_EPS) * g_ref[...]).astype(BF16)
    y = x
    for c in range(wg_ref.shape[-1] // cf):
        cs = slice(c * cf, (c + 1) * cf)
        gate = jnp.dot(h, wg_ref[:, cs], preferred_element_type=F32)
        up = jnp.dot(h, wu_ref[:, cs], preferred_element_type=F32)
        act = gate * jax.nn.sigmoid(gate) * up
        y = y + jnp.dot(act.astype(BF16), wd_ref[cs, :], preferred_element_type=F32)
    if has_final:
        ms = jnp.mean(y * y, axis=-1, keepdims=True)
        y = y * lax.rsqrt(ms + RMS_EPS) * fg_ref[...]
    o_ref[...] = y


def _ffn(x, g, wg, wu, wd, final_g=None):
    m, d = x.shape
    hid = wg.shape[1]
    tm = _pick(m, (512, 256, 128))
    args = [x, g.reshape(1, d).astype(F32), wg, wu, wd]
    specs = [pl.BlockSpec((tm, d), lambda i: (i, 0)), _resident((1, d)),
             _resident((d, hid)), _resident((d, hid)), _resident((hid, d))]
    if final_g is not None:
        args.append(final_g.reshape(1, d).astype(F32))
        specs.append(_resident((1, d)))
    return pl.pallas_call(
        functools.partial(_ffn_kernel, cf=_pick(hid, (256, 128)), has_final=final_g is not None),
        grid=(m // tm,),
        in_specs=specs,
        out_specs=pl.BlockSpec((tm, d), lambda i: (i, 0)),
        out_shape=jax.ShapeDtypeStruct((m, d), F32),
        compiler_params=_cparams(("parallel",)),
        name="ffn",
    )(*args)


def _dsa_mask_kernel(qi_ref, ki_ref, w_ref, o_ref, keys_ref, cand_ref, *, tq, tk, rb, nk, ksel, idx_scale):
    i = pl.program_id(1)
    groups = tk // LANES
    n_full = (i * tq) // tk
    off = i * tq - n_full * tk
    w = w_ref[0]
    kf = float(ksel)

    def score_chunk(j, diag):
        kc = ki_ref[0, j]
        sc = jnp.zeros((tq, tk), F32)
        for h in range(IDX_HEADS):
            d = jnp.dot(qi_ref[0, h], kc, preferred_element_type=F32)
            sc = sc + jnp.maximum(d, 0.0) * w[:, h:h + 1]
        sc = sc * idx_scale
        bits = lax.bitcast_convert_type(sc, jnp.int32)
        key = jnp.where(bits < 0, INT_MIN - bits, bits)
        if diag:
            row = lax.broadcasted_iota(jnp.int32, (tq, tk), 0)
            col = lax.broadcasted_iota(jnp.int32, (tq, tk), 1)
            key = jnp.where(col <= row + off, key, INT_MIN)
        keys_ref[j] = key

    def _plain(j, c):
        score_chunk(j, False)
        return c

    lax.fori_loop(0, n_full, _plain, 0)
    score_chunk(n_full, True)

    ones_l = jnp.ones((LANES, LANES), BF16)

    def count(strict):
        accs = []
        for r in range(tq // rb):
            def body(j, acc, r=r):
                kc = keys_ref[j, r * rb:(r + 1) * rb, :]
                cand = cand_ref[r * rb:(r + 1) * rb, :]
                for g in range(groups):
                    kg = kc[:, g * LANES:(g + 1) * LANES]
                    acc = acc + jnp.where(kg > cand if strict else kg >= cand, 1.0, 0.0)
                return acc
            accs.append(lax.fori_loop(0, n_full + 1, body, jnp.zeros((rb, LANES), F32)))
        return jnp.dot(jnp.concatenate(accs, axis=0).astype(BF16), ones_l, preferred_element_type=F32)

    cand_ref[...] = jnp.zeros((tq, LANES), jnp.int32)
    prefix = jnp.where(count(False) >= kf, 0, INT_MIN).astype(jnp.int32)

    def bit_body(b, prefix):
        cand = prefix + jnp.left_shift(jnp.int32(1), 30 - b)
        cand_ref[...] = cand
        return jnp.where(count(False) >= kf, cand, prefix)

    tau = lax.fori_loop(0, 31, bit_body, prefix)
    cand_ref[...] = tau
    need = kf - count(True)

    tri = (lax.broadcasted_iota(jnp.int32, (tk, tk), 0)
           <= lax.broadcasted_iota(jnp.int32, (tk, tk), 1)).astype(BF16)
    ones_t = jnp.ones((tk, LANES), BF16)
    tau_t = jnp.concatenate([tau] * groups, axis=1)
    need_t = jnp.concatenate([need] * groups, axis=1)

    def emit(j, carry):
        kc = keys_ref[j]
        eq = kc == tau_t
        eqf = jnp.where(eq, 1.0, 0.0).astype(BF16)
        rank = jnp.dot(eqf, tri, preferred_element_type=F32) + jnp.concatenate([carry] * groups, axis=1)
        sel = (kc > tau_t) | (eq & (rank <= need_t) & (kc > INT_MIN))
        o_ref[0, j] = jnp.where(sel, 0.0, NEG).astype(o_ref.dtype)
        return carry + jnp.dot(eqf, ones_t, preferred_element_type=F32)

    lax.fori_loop(0, n_full + 1, emit, jnp.zeros((tq, LANES), F32))

    def _fill(j, c):
        o_ref[0, j] = jnp.full((tq, tk), NEG, o_ref.dtype)
        return c

    lax.fori_loop(n_full + 1, nk, _fill, 0)


def _dsa_mask(qi3, ki3, w, tk, ksel):
    bsz, hi, s_len, d3 = qi3.shape
    nk = s_len // tk
    tq = tk
    assert s_len // LANES <= 256, "per-lane counts must stay exact in bf16"
    idx_scale = (IDX_DIM ** -0.5) * (IDX_HEADS ** -0.5)
    return pl.pallas_call(
        functools.partial(_dsa_mask_kernel, tq=tq, tk=tk, rb=_pick(tq, (256, 128)), nk=nk, ksel=ksel,
                          idx_scale=idx_scale),
        grid=(bsz, s_len // tq),
        in_specs=[pl.BlockSpec((1, hi, tq, d3), lambda b, i: (b, 0, i, 0)),
                  pl.BlockSpec((1, nk, d3, tk), lambda b, i: (b, 0, 0, 0)),
                  pl.BlockSpec((1, tq, hi), lambda b, i: (b, i, 0))],
        out_specs=pl.BlockSpec((1, nk, tq, tk), lambda b, i: (b, 0, i, 0)),
        out_shape=jax.ShapeDtypeStruct((bsz, nk, s_len, tk), BF16),
        scratch_shapes=[pltpu.VMEM((nk, tq, tk), jnp.int32), pltpu.VMEM((tq, LANES), jnp.int32)],
        compiler_params=_cparams(("parallel", "arbitrary")),
        name="dsa_mask",
    )(qi3, ki3, w)


def _finish_tail(s, band, causal):
    nr, nc = s.shape[0] // LANES, s.shape[1] // LANES
    keep = (lax.broadcasted_iota(jnp.int32, (LANES, LANES), 1)
            <= lax.broadcasted_iota(jnp.int32, (LANES, LANES), 0))
    rows = []
    for a in range(nr):
        blocks = []
        for b in range(nc):
            d = (nc - nr) + a - b
            blk = s[a * LANES:(a + 1) * LANES, b * LANES:(b + 1) * LANES]
            if band is not None and d in (0, 1):
                blk = blk + band[d]
            if causal and d == 0:
                blk = jnp.where(keep, blk, NEG)
            elif causal and d < 0:
                blk = jnp.full_like(blk, NEG)
            blocks.append(blk)
        rows.append(jnp.concatenate(blocks, axis=1))
    return jnp.concatenate(rows, axis=0)


def _flash_kernel(*refs, tile, wq, dv, v_shared, q_axis, has_kbias, has_band, has_mask, diff):
    it = iter(refs)
    q_ref, k_ref, v_ref = next(it), next(it), next(it)
    kb_ref = next(it) if has_kbias else None
    t_ref = next(it) if has_band else None
    mk_ref = next(it) if has_mask else None
    lam_ref = next(it) if diff else None
    sg_ref = next(it) if diff else None
    o_ref, m_ref, acc_ref = next(it), next(it), next(it)

    i = pl.program_id(q_axis)
    dvp = acc_ref.shape[-1]
    a_groups = dvp // LANES
    qblk = q_ref[0]
    if wq == LANES:
        lane = lax.broadcasted_iota(jnp.int32, qblk.shape, 1)
        q32 = qblk.astype(F32)
        qs = [jnp.where(lane < LANES // 2, q32, 0.0).astype(BF16),
              jnp.where(lane >= LANES // 2, q32, 0.0).astype(BF16)]
    else:
        qs = [qblk[:, :LANES], qblk[:, LANES:]]
    m_ref[...] = jnp.full_like(m_ref, NEG)
    acc_ref[...] = jnp.zeros_like(acc_ref)

    def step(j, width=1, tail=False):
        cols = width * tile
        off = pl.multiple_of(j * tile, tile)
        kc = k_ref[0, pl.ds(off, cols), :]
        vc = v_ref[0, pl.ds(off, cols), :]
        cat = lambda f: f(j) if width == 1 else jnp.concatenate([f(j + t) for t in range(width)], axis=1)
        mk = cat(lambda t: mk_ref[0, t]).astype(F32) if has_mask else None
        for a in range(2):
            ka = kc if wq == LANES else kc[:, a * LANES:(a + 1) * LANES]
            s = lax.dot_general(qs[a], ka, (((1,), (1,)), ((), ())), preferred_element_type=F32)
            if has_kbias:
                s = s + cat(lambda t: kb_ref[0, a, t])
            if tail:
                s = _finish_tail(s, (t_ref[a, 0], t_ref[a, 1]) if has_band else None, not has_mask)
            if has_mask:
                s = s + mk
            m_old = m_ref[a]
            m_new = jnp.maximum(m_old, jnp.max(s, axis=-1, keepdims=True))
            p = jnp.exp2(s - jnp.concatenate([m_new] * (cols // LANES), axis=1))
            alpha = jnp.exp2(m_old - m_new)
            va = vc if v_shared else vc[:, a * LANES:(a + 1) * LANES]
            acc_ref[a] = acc_ref[a] * jnp.concatenate([alpha] * a_groups, axis=1) + jnp.dot(
                p.astype(BF16), va, preferred_element_type=F32)
            m_ref[a] = m_new

    n_far = jnp.maximum(i - 1, 0)

    def _far(t, c):
        step(t * FAR_WIDTH, width=FAR_WIDTH)
        return c

    lax.fori_loop(0, n_far // FAR_WIDTH, _far, 0)
    w = FAR_WIDTH // 2
    while w >= 1:
        @pl.when(n_far % (2 * w) >= w)
        def _(w=w):
            step((n_far // (2 * w)) * (2 * w), width=w)
        w //= 2

    @pl.when(i >= 1)
    def _():
        step(i - 1, width=2, tail=True)

    @pl.when(i == 0)
    def _():
        step(0, tail=True)

    outs = []
    for a in range(2):
        acc = acc_ref[a]
        outs.append(acc[:, :dv] / acc[:, dv:dv + 1])
    if diff:
        d = outs[0] - lam_ref[...] * outs[1]
        ms = jnp.mean(d * d, axis=-1, keepdims=True)
        o_ref[0] = (d * lax.rsqrt(ms + RMS_EPS) * sg_ref[...]).astype(o_ref.dtype)
    else:
        o_ref[0] = jnp.concatenate(outs, axis=1).astype(o_ref.dtype)


def _flash(qa, ka, va, *, n_pairs, qoff, koff, voff, wq, dv, tile, v_shared=False, kbias=None,
           band=None, mask=None, diff=None):
    bsz, s_len, _ = qa.shape
    nk = s_len // tile
    dvp = 2 * LANES if v_shared else LANES
    assert qoff % wq == 0 and koff % wq == 0 and voff % (2 * LANES) == 0
    qb, kb_, vb = qoff // wq, koff // wq, voff // (2 * LANES)
    if mask is None:
        grid, q_axis = (bsz, n_pairs, nk), 2
        ix = lambda f: (lambda b, h, i: f(b, h, i))
    else:
        grid, q_axis = (bsz, nk, n_pairs), 1
        ix = lambda f: (lambda b, i, h: f(b, h, i))
    args = [qa, ka, va]
    specs = [pl.BlockSpec((1, tile, wq), ix(lambda b, h, i: (b, i, qb + h))),
             pl.BlockSpec((1, s_len, wq), ix(lambda b, h, i: (b, 0, kb_ + h))),
             pl.BlockSpec((1, s_len, 2 * LANES), ix(lambda b, h, i: (b, 0, vb + h)))]
    if kbias is not None:
        args.append(kbias)
        specs.append(pl.BlockSpec((1, 2, nk, 1, tile), ix(lambda b, h, i: (b, h, 0, 0, 0))))
    if band is not None:
        args.append(band)
        specs.append(pl.BlockSpec((2, 2, LANES, LANES), ix(lambda b, h, i: (h, 0, 0, 0))))
    if mask is not None:
        args.append(mask)
        specs.append(pl.BlockSpec((1, nk, tile, tile), ix(lambda b, h, i: (b, 0, i, 0))))
    if diff is not None:
        for t in diff:
            args.append(t.reshape(1, LANES).astype(F32))
            specs.append(pl.BlockSpec((1, LANES), ix(lambda b, h, i: (0, 0))))
    return pl.pallas_call(
        functools.partial(_flash_kernel, tile=tile, wq=wq, dv=dv, v_shared=v_shared, q_axis=q_axis,
                          has_kbias=kbias is not None, has_band=band is not None,
                          has_mask=mask is not None, diff=diff is not None),
        grid=grid,
        in_specs=specs,
        out_specs=pl.BlockSpec((1, tile, LANES), ix(lambda b, h, i: (b, i, h))),
        out_shape=jax.ShapeDtypeStruct((bsz, s_len, n_pairs * LANES), BF16),
        scratch_shapes=[pltpu.VMEM((2, tile, LANES), F32), pltpu.VMEM((2, tile, dvp), F32)],
        compiler_params=_cparams(("parallel", "parallel", "arbitrary")),
        name="flash",
    )(*args)


def _t5_bucket(dist):
    exact = T5_BUCKETS // 2
    d = np.maximum(dist, 1).astype(np.float32)
    log_b = exact + (np.log(d / np.float32(exact)) / np.float32(math.log(T5_MAX_DIST / exact))
                     * np.float32(T5_BUCKETS - exact)).astype(np.int32)
    log_b = np.minimum(log_b, T5_BUCKETS - 1)
    return np.where(dist < exact, dist, log_b).astype(np.int32)


def _band_tiles(t5_table):
    assert T5_MAX_DIST <= LANES
    r = np.arange(LANES)[:, None]
    c = np.arange(LANES)[None, :]
    bmap = np.stack([_t5_bucket(np.maximum(r - c, 0)), _t5_bucket(LANES + r - c)])
    t = t5_table.astype(F32)
    return jnp.transpose(t[bmap] - t[T5_BUCKETS - 1], (3, 0, 1, 2)) * LOG2E


def _with_ones_col(w, heads, d):
    k = w.shape[0]
    dvp = -(-(d + 1) // LANES) * LANES
    wp = jnp.pad(w.reshape(k, heads, d), ((0, 0), (0, 0), (0, dvp - d))).reshape(k, heads * dvp)
    bias = np.zeros((heads, dvp), np.float32)
    bias[:, d] = 1.0
    return wp, jnp.asarray(bias.reshape(-1))


def _hi_lo(a):
    hi = a.astype(BF16)
    lo = (a - hi.astype(F32)).astype(BF16)
    return hi, lo


def _pad_to(a, n, axis):
    pad = [(0, 0)] * a.ndim
    pad[axis] = (0, n - a.shape[axis])
    return jnp.pad(a, pad)


def _rope_tables(s_len):
    pos = jnp.arange(s_len, dtype=F32)
    inv = ROPE_THETA ** (-jnp.arange(0, MLA_ROPE, 2, dtype=F32) / MLA_ROPE)
    ang = pos[:, None] * inv[None, :]
    ang = jnp.concatenate([ang, ang], axis=-1)
    return jnp.cos(ang), jnp.sin(ang)


def _rot_half_matrix():
    half = MLA_ROPE // 2
    r = np.zeros((MLA_ROPE, MLA_ROPE), np.float32)
    for c in range(half):
        r[c + half, c] = -1.0
        r[c, c + half] = 1.0
    return jnp.asarray(r)


def _even_mixer(x2, bsz, s_len, g, w_in, b_forget, w_out, t5_table, tile):
    m, dm = x2.shape
    nk = s_len // tile
    scale = HEAD_DIM ** -0.5 * LOG2E
    wfq, wfk, wfv, wff, wdq, wdk, wdv, wiq, wik, wiw = jnp.split(
        w_in, [int(o) for o in np.cumsum(EVEN_SPLITS)[:-1]], axis=1)
    wfv1, ones_f = _with_ones_col(wfv, FOX_HEADS, HEAD_DIM)
    wdv1, ones_d = _with_ones_col(wdv, DSA_HEADS, HEAD_DIM)
    w_main = jnp.concatenate([wfq * scale, wfk, wfv1, wdq * scale, wdk, wdv1], axis=1).astype(BF16)
    zeros = lambda n: jnp.zeros((n,), F32)
    b_main = jnp.concatenate([zeros(2 * FOX_W), ones_f, zeros(2 * DSA_W), ones_d])
    qoff_f, koff_f, voff_f = 0, FOX_W, 2 * FOX_W
    qoff_d = voff_f + wfv1.shape[1]
    koff_d, voff_d = qoff_d + DSA_W, qoff_d + 2 * DSA_W
    main = _mm(x2, w_main, g=g, bias=b_main, out_dtype=BF16).reshape(bsz, s_len, -1)
    w_aux = jnp.concatenate([_pad_to(jnp.concatenate([wff, wiw], axis=1), LANES, 1),
                             _pad_to(wik, LANES, 1), wiq], axis=1).astype(BF16)
    aux = _mm(x2, w_aux, g=g).reshape(bsz, s_len, -1)
    ff, iw = aux[..., :FOX_HEADS], aux[..., FOX_HEADS:FOX_HEADS + IDX_HEADS]
    ik = aux[..., LANES:LANES + IDX_DIM]
    iq = aux[..., 2 * LANES:2 * LANES + IDX_HEADS * IDX_DIM]

    log_f = jax.nn.log_sigmoid(ff + b_forget.astype(F32))
    log_cum = jnp.cumsum(log_f, axis=1)
    kbias = (-LOG2E * jnp.transpose(log_cum, (0, 2, 1))).reshape(bsz, FOX_HEADS, nk, 1, tile)
    fox = _flash(main, main, main, n_pairs=FOX_HEADS // 2, qoff=qoff_f, koff=koff_f, voff=voff_f,
                 wq=LANES, dv=HEAD_DIM, tile=tile, kbias=kbias)

    q_hi, q_lo = _hi_lo(iq.reshape(bsz, s_len, IDX_HEADS, IDX_DIM))
    k_hi, k_lo = _hi_lo(ik)
    qi3 = jnp.transpose(jnp.concatenate([q_hi, q_hi, q_lo], axis=-1), (0, 2, 1, 3))
    ki3 = jnp.transpose(jnp.concatenate([k_hi, k_lo, k_hi], axis=-1).reshape(bsz, nk, tile, 3 * IDX_DIM),
                        (0, 1, 3, 2))
    mask = _dsa_mask(qi3, ki3, iw, tile, min(DSA_TOPK, s_len // 4))
    dsa = _flash(main, main, main, n_pairs=DSA_HEADS // 2, qoff=qoff_d, koff=koff_d, voff=voff_d,
                 wq=LANES, dv=HEAD_DIM, tile=tile, band=_band_tiles(t5_table), mask=mask)

    mixed = jnp.concatenate([fox, dsa], axis=-1).reshape(m, -1)
    return _mm(mixed, w_out.astype(BF16), res=x2)


def _odd_mixer(x2, bsz, s_len, g, w_in, lq1, lk1, lq2, lk2, subln_g, q_norm_g, w_uq, kv_norm_g, w_ukv,
               w_out, t5_table, lambda_init, tile):
    m, dm = x2.shape
    wcq, wck, wcv, wmq, wmkv, wmkr = jnp.split(
        w_in, [int(o) for o in np.cumsum(ODD_SPLITS)[:-1]], axis=1)
    wcv1, ones_c = _with_ones_col(wcv, DIFF_HEADS, DIFF_VDIM)
    w_main = jnp.concatenate([wcq * (HEAD_DIM ** -0.5 * LOG2E), wck, wcv1], axis=1).astype(BF16)
    b_main = jnp.concatenate([jnp.zeros((2 * DIFF_QK_W,), F32), ones_c])
    main = _mm(x2, w_main, g=g, bias=b_main, out_dtype=BF16).reshape(bsz, s_len, -1)
    lat_w = MLA_Q_RANK + 2 * LANES
    w_aux = _pad_to(jnp.concatenate([wmq, wmkv, wmkr], axis=1), lat_w, 1).astype(BF16)
    aux = _mm(x2, w_aux, g=g)

    lam = (jnp.exp(jnp.sum(lq1.astype(F32) * lk1.astype(F32)))
           - jnp.exp(jnp.sum(lq2.astype(F32) * lk2.astype(F32))) + lambda_init)
    diff = _flash(main, main, main, n_pairs=DIFF_HEADS, qoff=0, koff=DIFF_QK_W, voff=2 * DIFF_QK_W,
                  wq=LANES, dv=DIFF_VDIM, tile=tile, v_shared=True, band=_band_tiles(t5_table),
                  diff=(jnp.full((LANES,), lam, F32), subln_g.astype(F32) * (1.0 - lambda_init)))

    dqk = MLA_NOPE + MLA_ROPE
    cos, sin = _rope_tables(s_len)
    rmat = _rot_half_matrix()
    head_tab = lambda first, rope: jnp.concatenate(
        [jnp.full((s_len, MLA_NOPE), first, F32), rope, jnp.zeros((s_len, LANES - dqk), F32)], axis=1)
    c1, c2 = head_tab(1.0, cos), head_tab(0.0, sin)
    wq3 = w_uq.reshape(MLA_Q_RANK, MLA_HEADS, dqk)
    wq1 = _pad_to(wq3, LANES, 2).reshape(MLA_Q_RANK, -1)
    wq2 = jnp.pad(jnp.einsum('khr,rs->khs', wq3[..., MLA_NOPE:], rmat),
                  ((0, 0), (0, 0), (MLA_NOPE, LANES - dqk))).reshape(MLA_Q_RANK, -1)
    qscale = dqk ** -0.5 * LOG2E
    mla_q = _mm(aux, wq1.astype(BF16), xblk=0, kb=MLA_Q_RANK, g=q_norm_g, w2=wq2.astype(BF16),
                rot=(c1 * qscale, c2 * qscale), out_dtype=BF16).reshape(bsz, s_len, -1)
    wkv3 = w_ukv.reshape(MLA_KV_RANK, MLA_HEADS, MLA_NOPE + MLA_VDIM)
    wk_nope = _pad_to(wkv3[..., :MLA_NOPE], LANES, 2).reshape(MLA_KV_RANK, -1)
    wv1, ones_v = _with_ones_col(wkv3[..., MLA_NOPE:].reshape(MLA_KV_RANK, -1), MLA_HEADS, MLA_VDIM)
    eye_blk = lambda mat: jnp.tile(jnp.pad(mat, ((0, 0), (MLA_NOPE, LANES - dqk))), (1, MLA_HEADS))
    kw = MLA_HEADS * LANES
    top = jnp.concatenate([wk_nope, wv1], axis=1)
    mid1 = jnp.concatenate([eye_blk(jnp.eye(MLA_ROPE, dtype=F32)), jnp.zeros((MLA_ROPE, kw), F32)], axis=1)
    mid2 = jnp.concatenate([eye_blk(rmat), jnp.zeros((MLA_ROPE, kw), F32)], axis=1)
    wk1 = _pad_to(jnp.concatenate([top, mid1], axis=0), 2 * LANES, 0)
    wk2 = _pad_to(jnp.concatenate([jnp.zeros_like(top), mid2], axis=0), 2 * LANES, 0)
    ones_tab = jnp.ones((s_len, LANES), F32)
    g_kv = jnp.concatenate([kv_norm_g.astype(F32), jnp.ones((2 * LANES - MLA_KV_RANK,), F32)])
    mla_kv = _mm(aux, wk1.astype(BF16), xblk=MLA_Q_RANK // (2 * LANES), kb=2 * LANES, g=g_kv,
                 n_norm=MLA_KV_RANK, w2=wk2.astype(BF16),
                 rot=(jnp.concatenate([c1, ones_tab], axis=1), jnp.concatenate([c2, 0.0 * ones_tab], axis=1)),
                 bias=jnp.concatenate([jnp.zeros((kw,), F32), ones_v]),
                 out_dtype=BF16).reshape(bsz, s_len, -1)
    mla = _flash(mla_q, mla_kv, mla_kv, n_pairs=MLA_HEADS // 2, qoff=0, koff=0, voff=kw,
                 wq=2 * LANES, dv=MLA_VDIM, tile=tile)

    mixed = jnp.concatenate([diff, mla], axis=-1).reshape(m, -1)
    return _mm(mixed, w_out.astype(BF16), res=x2)


def kernel(x, norm_mix_g, norm_ffn_g, w_in_even, b_forget, w_out_even, w_in_odd, lambda_q1, lambda_k1, lambda_q2, lambda_k2, diff_subln_g, mla_q_norm_g, w_mla_uq, mla_kv_norm_g, w_mla_ukv, w_out_odd, t5_bias, w_ffn_gate, w_ffn_up, w_ffn_down, final_norm_g):
    bsz, s_len, dm = x.shape
    depth = norm_mix_g.shape[0]
    tile = _pick(s_len, (512, 256, 128))
    x2 = x.astype(F32).reshape(bsz * s_len, dm)
    for layer in range(depth):
        j = layer // 2
        if layer % 2 == 0:
            x2 = _even_mixer(x2, bsz, s_len, norm_mix_g[layer], w_in_even[j], b_forget[j], w_out_even[j],
                             t5_bias, tile)
        else:
            lambda_init = 0.8 - 0.6 * math.exp(-0.3 * layer)
            x2 = _odd_mixer(x2, bsz, s_len, norm_mix_g[layer], w_in_odd[j], lambda_q1[j], lambda_k1[j],
                            lambda_q2[j], lambda_k2[j], diff_subln_g[j], mla_q_norm_g[j], w_mla_uq[j],
                            mla_kv_norm_g[j], w_mla_ukv[j], w_out_odd[j], t5_bias, lambda_init, tile)
        last = layer == depth - 1
        x2 = _ffn(x2, norm_ffn_g[layer], w_ffn_gate[layer].astype(BF16), w_ffn_up[layer].astype(BF16),
                  w_ffn_down[layer].astype(BF16), final_g=final_norm_g if last else None)
    return x2.reshape(bsz, s_len, dm)
```

```python
import functools
import math

import numpy as np
import jax
import jax.numpy as jnp
from jax import lax
from jax.experimental import pallas as pl
from jax.experimental.pallas import tpu as pltpu

F32 = jnp.float32
BF16 = jnp.bfloat16

HEAD_DIM = 64
RMS_EPS = 1e-6
FOX_HEADS = 8
DSA_HEADS = 8
IDX_HEADS = 4
IDX_DIM = 64
DSA_TOPK = 256
DIFF_HEADS = 4
DIFF_VDIM = 2 * HEAD_DIM
MLA_HEADS = 8
MLA_NOPE = 64
MLA_ROPE = 32
MLA_VDIM = 64
MLA_Q_RANK = 256
MLA_KV_RANK = 128
ROPE_THETA = 10000.0
T5_BUCKETS = 32
T5_MAX_DIST = 128

FOX_W = FOX_HEADS * HEAD_DIM
DSA_W = DSA_HEADS * HEAD_DIM
EVEN_SPLITS = [FOX_W, FOX_W, FOX_W, FOX_HEADS, DSA_W, DSA_W, DSA_W,
               IDX_HEADS * IDX_DIM, IDX_DIM, IDX_HEADS]
DIFF_QK_W = DIFF_HEADS * 2 * HEAD_DIM
DIFF_V_W = DIFF_HEADS * DIFF_VDIM
ODD_SPLITS = [DIFF_QK_W, DIFF_QK_W, DIFF_V_W, MLA_Q_RANK, MLA_KV_RANK, MLA_ROPE]

LANES = 128
NEG = -1e30
INT_MIN = -(2 ** 31)
VMEM_LIMIT = 48 * 1024 * 1024
FAR_WIDTH = 4
LOG2E = math.log2(math.e)


def _cparams(sem):
    return pltpu.CompilerParams(dimension_semantics=sem, vmem_limit_bytes=VMEM_LIMIT)


def _pick(n, prefs):
    for p in prefs:
        if n % p == 0:
            return p
    return n


def _resident(shape):
    return pl.BlockSpec(shape, lambda i: (0,) * len(shape), pipeline_mode=pl.Buffered(1))


def _mm_kernel(*refs, n_norm, cw, has_g, has_w2, has_rot, has_bias, has_res):
    it = iter(refs)
    x_ref = next(it)
    g_ref = next(it) if has_g else None
    w_ref = next(it)
    w2_ref = next(it) if has_w2 else None
    c1_ref = next(it) if has_rot else None
    c2_ref = next(it) if has_rot else None
    b_ref = next(it) if has_bias else None
    r_ref = next(it) if has_res else None
    o_ref = next(it)

    x = x_ref[...].astype(F32)
    if has_g:
        kb = x.shape[-1]
        if n_norm == kb:
            ms = jnp.mean(x * x, axis=-1, keepdims=True)
            x = x * lax.rsqrt(ms + RMS_EPS) * g_ref[...]
        else:
            normed = lax.broadcasted_iota(jnp.int32, x.shape, 1) < n_norm
            xs = jnp.where(normed, x, 0.0)
            ms = jnp.sum(xs * xs, axis=-1, keepdims=True) * (1.0 / n_norm)
            x = x * jnp.where(normed, lax.rsqrt(ms + RMS_EPS), 1.0) * g_ref[...]
    xn = x.astype(BF16)

    for c in range(o_ref.shape[-1] // cw):
        cs = slice(c * cw, (c + 1) * cw)
        acc = jnp.dot(xn, w_ref[:, cs], preferred_element_type=F32)
        if has_rot:
            r = c1_ref.shape[-1] // LANES
            ts = slice((c % r) * LANES, (c % r + 1) * LANES)
            acc = acc * jnp.concatenate([c1_ref[:, ts]] * (cw // LANES), axis=1)
            acc = acc + jnp.dot(xn, w2_ref[:, cs], preferred_element_type=F32) * jnp.concatenate(
                [c2_ref[:, ts]] * (cw // LANES), axis=1)
        if has_bias:
            acc = acc + b_ref[:, cs]
        if has_res:
            acc = acc + r_ref[:, cs]
        o_ref[:, cs] = acc.astype(o_ref.dtype)


def _mm(x, w, *, xblk=0, kb=None, g=None, n_norm=None, w2=None, rot=None, bias=None, res=None,
        out_dtype=F32):
    m = x.shape[0]
    kb = x.shape[1] if kb is None else kb
    n = w.shape[1]
    tm = _pick(m, (512, 256, 128))
    r = 1 if rot is None else rot[0].shape[1] // LANES
    cw = n // r if rot is not None else _pick(n, (1024, 768, 512, 384, 256, 128))
    args = [x]
    specs = [pl.BlockSpec((tm, kb), lambda i: (i, xblk))]
    if g is not None:
        n_norm = kb if n_norm is None else n_norm
        args.append(g.reshape(1, kb).astype(F32))
        specs.append(_resident((1, kb)))
    args.append(w)
    specs.append(_resident((kb, n)))
    if w2 is not None:
        args.append(w2)
        specs.append(_resident((kb, n)))
    if rot is not None:
        s_len = rot[0].shape[0]
        assert s_len % tm == 0
        for c in rot:
            args.append(c)
            specs.append(pl.BlockSpec((tm, r * LANES), lambda i: (i % (s_len // tm), 0)))
    if bias is not None:
        args.append(bias.reshape(1, n).astype(F32))
        specs.append(_resident((1, n)))
    if res is not None:
        args.append(res)
        specs.append(pl.BlockSpec((tm, n), lambda i: (i, 0)))
    return pl.pallas_call(
        functools.partial(_mm_kernel, n_norm=n_norm, cw=cw, has_g=g is not None, has_w2=w2 is not None,
                          has_rot=rot is not None, has_bias=bias is not None, has_res=res is not None),
        grid=(m // tm,),
        in_specs=specs,
        out_specs=pl.BlockSpec((tm, n), lambda i: (i, 0)),
        out_shape=jax.ShapeDtypeStruct((m, n), out_dtype),
        compiler_params=_cparams(("parallel",)),
        name="mm",
    )(*args)


def _ffn_kernel(*refs, cf, has_final):
    it = iter(refs)
    x_ref, g_ref, wg_ref, wu_ref, wd_ref = (next(it) for _ in range(5))
    fg_ref = next(it) if has_final else None
    o_ref = next(it)

    x = x_ref[...]
    ms = jnp.mean(x * x, axis=-1, keepdims=True)
    h = (x * lax.rsqrt(ms + RMS_EPS) * g_ref[...]).astype(BF16)
    y = x
    for c in range(wg_ref.shape[-1] // cf):
        cs = slice(c * cf, (c + 1) * cf)
        gate = jnp.dot(h, wg_ref[:, cs], preferred_element_type=F32)
        up = jnp.dot(h, wu_ref[:, cs], preferred_element_type=F32)
        act = gate * jax.nn.sigmoid(gate) * up
        y = y + jnp.dot(act.astype(BF16), wd_ref[cs, :], preferred_element_type=F32)
    if has_final:
        ms = jnp.mean(y * y, axis=-1, keepdims=True)
        y = y * lax.rsqrt(ms + RMS_EPS) * fg_ref[...]
    o_ref[...] = y


def _ffn(x, g, wg, wu, wd, final_g=None):
    m, d = x.shape
    hid = wg.shape[1]
    tm = _pick(m, (512, 256, 128))
    args = [x, g.reshape(1, d).astype(F32), wg, wu, wd]
    specs = [pl.BlockSpec((tm, d), lambda i: (i, 0)), _resident((1, d)),
             _resident((d, hid)), _resident((d, hid)), _resident((hid, d))]
    if final_g is not None:
        args.append(final_g.reshape(1, d).astype(F32))
        specs.append(_resident((1, d)))
    return pl.pallas_call(
        functools.partial(_ffn_kernel, cf=_pick(hid, (256, 128)), has_final=final_g is not None),
        grid=(m // tm,),
        in_specs=specs,
        out_specs=pl.BlockSpec((tm, d), lambda i: (i, 0)),
        out_shape=jax.ShapeDtypeStruct((m, d), F32),
        compiler_params=_cparams(("parallel",)),
        name="ffn",
    )(*args)


def _dsa_mask_kernel(qt_ref, ki_ref, w_ref, o_ref, keys_ref, *, tq, tk, nk, ksel, idx_scale):
    i = pl.program_id(1)
    sub = 8
    n_full = (i * tq) // tk
    off = i * tq - n_full * tk
    w = w_ref[0]
    kf = float(ksel)
    split = lambda a: a.reshape(tk // sub, sub, tq)

    def score_chunk(j, diag):
        kc = ki_ref[0, j]
        sc = jnp.zeros((tk, tq), F32)
        for h in range(IDX_HEADS):
            d = jnp.dot(kc, qt_ref[0, h], preferred_element_type=F32)
            sc = sc + jnp.maximum(d, 0.0) * w[h:h + 1, :]
        sc = sc * idx_scale
        bits = lax.bitcast_convert_type(sc, jnp.int32)
        key = jnp.where(bits < 0, INT_MIN - bits, bits)
        if diag:
            krow = lax.broadcasted_iota(jnp.int32, (tk, tq), 0)
            qcol = lax.broadcasted_iota(jnp.int32, (tk, tq), 1)
            key = jnp.where(krow <= qcol + off, key, INT_MIN)
        keys_ref[j] = key

    def _plain(j, c):
        score_chunk(j, False)
        return c

    lax.fori_loop(0, n_full, _plain, 0)
    score_chunk(n_full, True)

    def count(cand, strict):
        def body(j, acc):
            kc = split(keys_ref[j])
            hit = kc > cand[None] if strict else kc >= cand[None]
            return acc + jnp.sum(jnp.where(hit, 1.0, 0.0), axis=0)
        acc = lax.fori_loop(0, n_full + 1, body, jnp.zeros((sub, tq), F32))
        return jnp.broadcast_to(jnp.sum(acc, axis=0, keepdims=True), (sub, tq))

    zero = jnp.zeros((sub, tq), jnp.int32)
    prefix = jnp.where(count(zero, False) >= kf, 0, INT_MIN).astype(jnp.int32)

    def bit_body(b, prefix):
        cand = prefix + jnp.left_shift(jnp.int32(1), 30 - b)
        return jnp.where(count(cand, False) >= kf, cand, prefix)

    tau = lax.fori_loop(0, 31, bit_body, prefix)
    need = kf - count(tau, True)

    tri = (lax.broadcasted_iota(jnp.int32, (tk, tk), 1)
           <= lax.broadcasted_iota(jnp.int32, (tk, tk), 0)).astype(BF16)
    ones_s = jnp.ones((sub, tk), BF16)

    def emit(j, carry):
        kc = split(keys_ref[j])
        eq = kc == tau[None]
        eqf = jnp.where(eq, 1.0, 0.0).reshape(tk, tq).astype(BF16)
        rank = split(jnp.dot(tri, eqf, preferred_element_type=F32)) + carry[None]
        sel = (kc > tau[None]) | (eq & (rank <= need[None]) & (kc > INT_MIN))
        o_ref[0, j] = jnp.where(sel, 0.0, NEG).reshape(tk, tq).T.astype(o_ref.dtype)
        return carry + jnp.dot(ones_s, eqf, preferred_element_type=F32)

    lax.fori_loop(0, n_full + 1, emit, jnp.zeros((sub, tq), F32))

    def _fill(j, c):
        o_ref[0, j] = jnp.full((tq, tk), NEG, o_ref.dtype)
        return c

    lax.fori_loop(n_full + 1, nk, _fill, 0)


def _dsa_mask(iq, ik, iw, tk, ksel):
    bsz, s_len, hi, di = iq.shape
    nk = s_len // tk
    tq = tk
    q_hi, q_lo = _hi_lo(iq)
    k_hi, k_lo = _hi_lo(ik)
    qt = jnp.transpose(jnp.concatenate([q_hi, q_hi, q_lo], axis=-1), (0, 2, 3, 1))
    ki = jnp.concatenate([k_hi, k_lo, k_hi], axis=-1).reshape(bsz, nk, tk, 3 * di)
    wt = jnp.transpose(iw, (0, 2, 1))
    idx_scale = (di ** -0.5) * (hi ** -0.5)
    return pl.pallas_call(
        functools.partial(_dsa_mask_kernel, tq=tq, tk=tk, nk=nk, ksel=ksel, idx_scale=idx_scale),
        grid=(bsz, s_len // tq),
        in_specs=[pl.BlockSpec((1, hi, 3 * di, tq), lambda b, i: (b, 0, 0, i)),
                  pl.BlockSpec((1, nk, tk, 3 * di), lambda b, i: (b, 0, 0, 0)),
                  pl.BlockSpec((1, hi, tq), lambda b, i: (b, 0, i))],
        out_specs=pl.BlockSpec((1, nk, tq, tk), lambda b, i: (b, 0, i, 0)),
        out_shape=jax.ShapeDtypeStruct((bsz, nk, s_len, tk), BF16),
        scratch_shapes=[pltpu.VMEM((nk, tk, tq), jnp.int32)],
        compiler_params=_cparams(("parallel", "arbitrary")),
        name="dsa_mask",
    )(qt, ki, wt)


def _finish_tail(s, band, causal):
    nr, nc = s.shape[0] // LANES, s.shape[1] // LANES
    keep = (lax.broadcasted_iota(jnp.int32, (LANES, LANES), 1)
            <= lax.broadcasted_iota(jnp.int32, (LANES, LANES), 0))
    rows = []
    for a in range(nr):
        blocks = []
        for b in range(nc):
            d = (nc - nr) + a - b
            blk = s[a * LANES:(a + 1) * LANES, b * LANES:(b + 1) * LANES]
            if band is not None and d in (0, 1):
                blk = blk + band[d]
            if causal and d == 0:
                blk = jnp.where(keep, blk, NEG)
            elif causal and d < 0:
                blk = jnp.full_like(blk, NEG)
            blocks.append(blk)
        rows.append(jnp.concatenate(blocks, axis=1))
    return jnp.concatenate(rows, axis=0)


def _flash_kernel(*refs, tile, wq, dv, v_shared, q_axis, has_kbias, has_band, has_mask, diff):
    it = iter(refs)
    q_ref, k_ref, v_ref = next(it), next(it), next(it)
    kb_ref = next(it) if has_kbias else None
    t_ref = next(it) if has_band else None
    mk_ref = next(it) if has_mask else None
    lam_ref = next(it) if diff else None
    sg_ref = next(it) if diff else None
    o_ref, m_ref, acc_ref = next(it), next(it), next(it)

    i = pl.program_id(q_axis)
    dvp = acc_ref.shape[-1]
    a_groups = dvp // LANES
    qblk = q_ref[0]
    if wq == LANES:
        lane = lax.broadcasted_iota(jnp.int32, qblk.shape, 1)
        q32 = qblk.astype(F32)
        qs = [jnp.where(lane < LANES // 2, q32, 0.0).astype(BF16),
              jnp.where(lane >= LANES // 2, q32, 0.0).astype(BF16)]
    else:
        qs = [qblk[:, :LANES], qblk[:, LANES:]]
    m_ref[...] = jnp.full_like(m_ref, NEG)
    acc_ref[...] = jnp.zeros_like(acc_ref)

    def step(j, width=1, tail=False):
        cols = width * tile
        off = pl.multiple_of(j * tile, tile)
        kc = k_ref[0, pl.ds(off, cols), :]
        vc = v_ref[0, pl.ds(off, cols), :]
        cat = lambda f: f(j) if width == 1 else jnp.concatenate([f(j + t) for t in range(width)], axis=1)
        mk = cat(lambda t: mk_ref[0, t]).astype(F32) if has_mask else None
        for a in range(2):
            ka = kc if wq == LANES else kc[:, a * LANES:(a + 1) * LANES]
            s = lax.dot_general(qs[a], ka, (((1,), (1,)), ((), ())), preferred_element_type=F32)
            if has_kbias:
                s = s + cat(lambda t: kb_ref[0, a, t])
            if tail:
                s = _finish_tail(s, (t_ref[a, 0], t_ref[a, 1]) if has_band else None, not has_mask)
            if has_mask:
                s = s + mk
            m_old = m_ref[a]
            m_new = jnp.maximum(m_old, jnp.max(s, axis=-1, keepdims=True))
            p = jnp.exp2(s - jnp.concatenate([m_new] * (cols // LANES), axis=1))
            alpha = jnp.exp2(m_old - m_new)
            va = vc if v_shared else vc[:, a * LANES:(a + 1) * LANES]
            acc_ref[a] = acc_ref[a] * jnp.concatenate([alpha] * a_groups, axis=1) + jnp.dot(
                p.astype(BF16), va, preferred_element_type=F32)
            m_ref[a] = m_new

    n_far = jnp.maximum(i - 1, 0)

    def _far(t, c):
        step(t * FAR_WIDTH, width=FAR_WIDTH)
        return c

    lax.fori_loop(0, n_far // FAR_WIDTH, _far, 0)
    w = FAR_WIDTH // 2
    while w >= 1:
        @pl.when(n_far % (2 * w) >= w)
        def _(w=w):
            step((n_far // (2 * w)) * (2 * w), width=w)
        w //= 2

    @pl.when(i >= 1)
    def _():
        step(i - 1, width=2, tail=True)

    @pl.when(i == 0)
    def _():
        step(0, tail=True)

    outs = []
    for a in range(2):
        acc = acc_ref[a]
        outs.append(acc[:, :dv] / acc[:, dv:dv + 1])
    if diff:
        d = outs[0] - lam_ref[...] * outs[1]
        ms = jnp.mean(d * d, axis=-1, keepdims=True)
        o_ref[0] = (d * lax.rsqrt(ms + RMS_EPS) * sg_ref[...]).astype(o_ref.dtype)
    else:
        o_ref[0] = jnp.concatenate(outs, axis=1).astype(o_ref.dtype)


def _flash(qa, ka, va, *, n_pairs, qoff, koff, voff, wq, dv, tile, v_shared=False, kbias=None,
           band=None, mask=None, diff=None):
    bsz, s_len, _ = qa.shape
    nk = s_len // tile
    dvp = 2 * LANES if v_shared else LANES
    assert qoff % wq == 0 and koff % wq == 0 and voff % (2 * LANES) == 0
    qb, kb_, vb = qoff // wq, koff // wq, voff // (2 * LANES)
    if mask is None:
        grid, q_axis = (bsz, n_pairs, nk), 2
        ix = lambda f: (lambda b, h, i: f(b, h, i))
    else:
        grid, q_axis = (bsz, nk, n_pairs), 1
        ix = lambda f: (lambda b, i, h: f(b, h, i))
    args = [qa, ka, va]
    specs = [pl.BlockSpec((1, tile, wq), ix(lambda b, h, i: (b, i, qb + h))),
             pl.BlockSpec((1, s_len, wq), ix(lambda b, h, i: (b, 0, kb_ + h))),
             pl.BlockSpec((1, s_len, 2 * LANES), ix(lambda b, h, i: (b, 0, vb + h)))]
    if kbias is not None:
        args.append(kbias)
        specs.append(pl.BlockSpec((1, 2, nk, 1, tile), ix(lambda b, h, i: (b, h, 0, 0, 0))))
    if band is not None:
        args.append(band)
        specs.append(pl.BlockSpec((2, 2, LANES, LANES), ix(lambda b, h, i: (h, 0, 0, 0))))
    if mask is not None:
        args.append(mask)
        specs.append(pl.BlockSpec((1, nk, tile, tile), ix(lambda b, h, i: (b, 0, i, 0))))
    if diff is not None:
        for t in diff:
            args.append(t.reshape(1, LANES).astype(F32))
            specs.append(pl.BlockSpec((1, LANES), ix(lambda b, h, i: (0, 0))))
    return pl.pallas_call(
        functools.partial(_flash_kernel, tile=tile, wq=wq, dv=dv, v_shared=v_shared, q_axis=q_axis,
                          has_kbias=kbias is not None, has_band=band is not None,
                          has_mask=mask is not None, diff=diff is not None),
        grid=grid,
        in_specs=specs,
        out_specs=pl.BlockSpec((1, tile, LANES), ix(lambda b, h, i: (b, i, h))),
        out_shape=jax.ShapeDtypeStruct((bsz, s_len, n_pairs * LANES), BF16),
        scratch_shapes=[pltpu.VMEM((2, tile, LANES), F32), pltpu.VMEM((2, tile, dvp), F32)],
        compiler_params=_cparams(("parallel", "parallel", "arbitrary")),
        name="flash",
    )(*args)


def _t5_bucket(dist):
    exact = T5_BUCKETS // 2
    d = np.maximum(dist, 1).astype(np.float32)
    log_b = exact + (np.log(d / np.float32(exact)) / np.float32(math.log(T5_MAX_DIST / exact))
                     * np.float32(T5_BUCKETS - exact)).astype(np.int32)
    log_b = np.minimum(log_b, T5_BUCKETS - 1)
    return np.where(dist < exact, dist, log_b).astype(np.int32)


def _band_tiles(t5_table):
    assert T5_MAX_DIST <= LANES
    r = np.arange(LANES)[:, None]
    c = np.arange(LANES)[None, :]
    bmap = np.stack([_t5_bucket(np.maximum(r - c, 0)), _t5_bucket(LANES + r - c)])
    t = t5_table.astype(F32)
    return jnp.transpose(t[bmap] - t[T5_BUCKETS - 1], (3, 0, 1, 2)) * LOG2E


def _with_ones_col(w, heads, d):
    k = w.shape[0]
    dvp = -(-(d + 1) // LANES) * LANES
    wp = jnp.pad(w.reshape(k, heads, d), ((0, 0), (0, 0), (0, dvp - d))).reshape(k, heads * dvp)
    bias = np.zeros((heads, dvp), np.float32)
    bias[:, d] = 1.0
    return wp, jnp.asarray(bias.reshape(-1))


def _hi_lo(a):
    hi = a.astype(BF16)
    lo = (a - hi.astype(F32)).astype(BF16)
    return hi, lo


def _pad_to(a, n, axis):
    pad = [(0, 0)] * a.ndim
    pad[axis] = (0, n - a.shape[axis])
    return jnp.pad(a, pad)


def _rope_tables(s_len):
    pos = jnp.arange(s_len, dtype=F32)
    inv = ROPE_THETA ** (-jnp.arange(0, MLA_ROPE, 2, dtype=F32) / MLA_ROPE)
    ang = pos[:, None] * inv[None, :]
    ang = jnp.concatenate([ang, ang], axis=-1)
    return jnp.cos(ang), jnp.sin(ang)


def _rot_half_matrix():
    half = MLA_ROPE // 2
    r = np.zeros((MLA_ROPE, MLA_ROPE), np.float32)
    for c in range(half):
        r[c + half, c] = -1.0
        r[c, c + half] = 1.0
    return jnp.asarray(r)


def _even_mixer(x2, bsz, s_len, g, w_in, b_forget, w_out, t5_table, tile):
    m, dm = x2.shape
    nk = s_len // tile
    scale = HEAD_DIM ** -0.5 * LOG2E
    wfq, wfk, wfv, wff, wdq, wdk, wdv, wiq, wik, wiw = jnp.split(
        w_in, [int(o) for o in np.cumsum(EVEN_SPLITS)[:-1]], axis=1)
    wfv1, ones_f = _with_ones_col(wfv, FOX_HEADS, HEAD_DIM)
    wdv1, ones_d = _with_ones_col(wdv, DSA_HEADS, HEAD_DIM)
    w_main = jnp.concatenate([wfq * scale, wfk, wfv1, wdq * scale, wdk, wdv1], axis=1).astype(BF16)
    zeros = lambda n: jnp.zeros((n,), F32)
    b_main = jnp.concatenate([zeros(2 * FOX_W), ones_f, zeros(2 * DSA_W), ones_d])
    qoff_f, koff_f, voff_f = 0, FOX_W, 2 * FOX_W
    qoff_d = voff_f + wfv1.shape[1]
    koff_d, voff_d = qoff_d + DSA_W, qoff_d + 2 * DSA_W
    main = _mm(x2, w_main, g=g, bias=b_main, out_dtype=BF16).reshape(bsz, s_len, -1)
    w_aux = jnp.concatenate([_pad_to(jnp.concatenate([wff, wiw], axis=1), LANES, 1),
                             _pad_to(wik, LANES, 1), wiq], axis=1).astype(BF16)
    aux = _mm(x2, w_aux, g=g).reshape(bsz, s_len, -1)
    ff, iw = aux[..., :FOX_HEADS], aux[..., FOX_HEADS:FOX_HEADS + IDX_HEADS]
    ik = aux[..., LANES:LANES + IDX_DIM]
    iq = aux[..., 2 * LANES:2 * LANES + IDX_HEADS * IDX_DIM]

    log_f = jax.nn.log_sigmoid(ff + b_forget.astype(F32))
    log_cum = jnp.cumsum(log_f, axis=1)
    kbias = (-LOG2E * jnp.transpose(log_cum, (0, 2, 1))).reshape(bsz, FOX_HEADS, nk, 1, tile)
    fox = _flash(main, main, main, n_pairs=FOX_HEADS // 2, qoff=qoff_f, koff=koff_f, voff=voff_f,
                 wq=LANES, dv=HEAD_DIM, tile=tile, kbias=kbias)

    mask = _dsa_mask(iq.reshape(bsz, s_len, IDX_HEADS, IDX_DIM), ik, iw, tile, min(DSA_TOPK, s_len // 4))
    dsa = _flash(main, main, main, n_pairs=DSA_HEADS // 2, qoff=qoff_d, koff=koff_d, voff=voff_d,
                 wq=LANES, dv=HEAD_DIM, tile=tile, band=_band_tiles(t5_table), mask=mask)

    mixed = jnp.concatenate([fox, dsa], axis=-1).reshape(m, -1)
    return _mm(mixed, w_out.astype(BF16), res=x2)


def _odd_mixer(x2, bsz, s_len, g, w_in, lq1, lk1, lq2, lk2, subln_g, q_norm_g, w_uq, kv_norm_g, w_ukv,
               w_out, t5_table, lambda_init, tile):
    m, dm = x2.shape
    wcq, wck, wcv, wmq, wmkv, wmkr = jnp.split(
        w_in, [int(o) for o in np.cumsum(ODD_SPLITS)[:-1]], axis=1)
    wcv1, ones_c = _with_ones_col(wcv, DIFF_HEADS, DIFF_VDIM)
    w_main = jnp.concatenate([wcq * (HEAD_DIM ** -0.5 * LOG2E), wck, wcv1], axis=1).astype(BF16)
    b_main = jnp.concatenate([jnp.zeros((2 * DIFF_QK_W,), F32), ones_c])
    main = _mm(x2, w_main, g=g, bias=b_main, out_dtype=BF16).reshape(bsz, s_len, -1)
    lat_w = MLA_Q_RANK + 2 * LANES
    w_aux = _pad_to(jnp.concatenate([wmq, wmkv, wmkr], axis=1), lat_w, 1).astype(BF16)
    aux = _mm(x2, w_aux, g=g)

    lam = (jnp.exp(jnp.sum(lq1.astype(F32) * lk1.astype(F32)))
           - jnp.exp(jnp.sum(lq2.astype(F32) * lk2.astype(F32))) + lambda_init)
    diff = _flash(main, main, main, n_pairs=DIFF_HEADS, qoff=0, koff=DIFF_QK_W, voff=2 * DIFF_QK_W,
                  wq=LANES, dv=DIFF_VDIM, tile=tile, v_shared=True, band=_band_tiles(t5_table),
                  diff=(jnp.full((LANES,), lam, F32), subln_g.astype(F32) * (1.0 - lambda_init)))

    dqk = MLA_NOPE + MLA_ROPE
    cos, sin = _rope_tables(s_len)
    rmat = _rot_half_matrix()
    head_tab = lambda first, rope: jnp.concatenate(
        [jnp.full((s_len, MLA_NOPE), first, F32), rope, jnp.zeros((s_len, LANES - dqk), F32)], axis=1)
    c1, c2 = head_tab(1.0, cos), head_tab(0.0, sin)
    wq3 = w_uq.reshape(MLA_Q_RANK, MLA_HEADS, dqk)
    wq1 = _pad_to(wq3, LANES, 2).reshape(MLA_Q_RANK, -1)
    wq2 = jnp.pad(jnp.einsum('khr,rs->khs', wq3[..., MLA_NOPE:], rmat),
                  ((0, 0), (0, 0), (MLA_NOPE, LANES - dqk))).reshape(MLA_Q_RANK, -1)
    qscale = dqk ** -0.5 * LOG2E
    mla_q = _mm(aux, wq1.astype(BF16), xblk=0, kb=MLA_Q_RANK, g=q_norm_g, w2=wq2.astype(BF16),
                rot=(c1 * qscale, c2 * qscale), out_dtype=BF16).reshape(bsz, s_len, -1)
    wkv3 = w_ukv.reshape(MLA_KV_RANK, MLA_HEADS, MLA_NOPE + MLA_VDIM)
    wk_nope = _pad_to(wkv3[..., :MLA_NOPE], LANES, 2).reshape(MLA_KV_RANK, -1)
    wv1, ones_v = _with_ones_col(wkv3[..., MLA_NOPE:].reshape(MLA_KV_RANK, -1), MLA_HEADS, MLA_VDIM)
    eye_blk = lambda mat: jnp.tile(jnp.pad(mat, ((0, 0), (MLA_NOPE, LANES - dqk))), (1, MLA_HEADS))
    kw = MLA_HEADS * LANES
    top = jnp.concatenate([wk_nope, wv1], axis=1)
    mid1 = jnp.concatenate([eye_blk(jnp.eye(MLA_ROPE, dtype=F32)), jnp.zeros((MLA_ROPE, kw), F32)], axis=1)
    mid2 = jnp.concatenate([eye_blk(rmat), jnp.zeros((MLA_ROPE, kw), F32)], axis=1)
    wk1 = _pad_to(jnp.concatenate([top, mid1], axis=0), 2 * LANES, 0)
    wk2 = _pad_to(jnp.concatenate([jnp.zeros_like(top), mid2], axis=0), 2 * LANES, 0)
    ones_tab = jnp.ones((s_len, LANES), F32)
    g_kv = jnp.concatenate([kv_norm_g.astype(F32), jnp.ones((2 * LANES - MLA_KV_RANK,), F32)])
    mla_kv = _mm(aux, wk1.astype(BF16), xblk=MLA_Q_RANK // (2 * LANES), kb=2 * LANES, g=g_kv,
                 n_norm=MLA_KV_RANK, w2=wk2.astype(BF16),
                 rot=(jnp.concatenate([c1, ones_tab], axis=1), jnp.concatenate([c2, 0.0 * ones_tab], axis=1)),
                 bias=jnp.concatenate([jnp.zeros((kw,), F32), ones_v]),
                 out_dtype=BF16).reshape(bsz, s_len, -1)
    mla = _flash(mla_q, mla_kv, mla_kv, n_pairs=MLA_HEADS // 2, qoff=0, koff=0, voff=kw,
                 wq=2 * LANES, dv=MLA_VDIM, tile=tile)

    mixed = jnp.concatenate([diff, mla], axis=-1).reshape(m, -1)
    return _mm(mixed, w_out.astype(BF16), res=x2)


def kernel(x, norm_mix_g, norm_ffn_g, w_in_even, b_forget, w_out_even, w_in_odd, lambda_q1, lambda_k1, lambda_q2, lambda_k2, diff_subln_g, mla_q_norm_g, w_mla_uq, mla_kv_norm_g, w_mla_ukv, w_out_odd, t5_bias, w_ffn_gate, w_ffn_up, w_ffn_down, final_norm_g):
    bsz, s_len, dm = x.shape
    depth = norm_mix_g.shape[0]
    tile = _pick(s_len, (512, 256, 128))
    x2 = x.astype(F32).reshape(bsz * s_len, dm)
    for layer in range(depth):
        j = layer // 2
        if layer % 2 == 0:
            x2 = _even_mixer(x2, bsz, s_len, norm_mix_g[layer], w_in_even[j], b_forget[j], w_out_even[j],
                             t5_bias, tile)
        else:
            lambda_init = 0.8 - 0.6 * math.exp(-0.3 * layer)
            x2 = _odd_mixer(x2, bsz, s_len, norm_mix_g[layer], w_in_odd[j], lambda_q1[j], lambda_k1[j],
                            lambda_q2[j], lambda_k2[j], diff_subln_g[j], mla_q_norm_g[j], w_mla_uq[j],
                            mla_kv_norm_g[j], w_mla_ukv[j], w_out_odd[j], t5_bias, lambda_init, tile)
        last = layer == depth - 1
        x2 = _ffn(x2, norm_ffn_g[layer], w_ffn_gate[layer].astype(BF16), w_ffn_up[layer].astype(BF16),
                  w_ffn_down[layer].astype(BF16), final_g=final_norm_g if last else None)
    return x2.reshape(bsz, s_len, dm)
```

```python
import functools
import math

import numpy as np
import jax
import jax.numpy as jnp
from jax import lax
from jax.experimental import pallas as pl
from jax.experimental.pallas import tpu as pltpu

F32 = jnp.float32
BF16 = jnp.bfloat16

HEAD_DIM = 64
RMS_EPS = 1e-6
FOX_HEADS = 8
DSA_HEADS = 8
IDX_HEADS = 4
IDX_DIM = 64
DSA_TOPK = 256
DIFF_HEADS = 4
DIFF_VDIM = 2 * HEAD_DIM
MLA_HEADS = 8
MLA_NOPE = 64
MLA_ROPE = 32
MLA_VDIM = 64
MLA_Q_RANK = 256
MLA_KV_RANK = 128
ROPE_THETA = 10000.0
T5_BUCKETS = 32
T5_MAX_DIST = 128

FOX_W = FOX_HEADS * HEAD_DIM
DSA_W = DSA_HEADS * HEAD_DIM
EVEN_SPLITS = [FOX_W, FOX_W, FOX_W, FOX_HEADS, DSA_W, DSA_W, DSA_W,
               IDX_HEADS * IDX_DIM, IDX_DIM, IDX_HEADS]
DIFF_QK_W = DIFF_HEADS * 2 * HEAD_DIM
DIFF_V_W = DIFF_HEADS * DIFF_VDIM
ODD_SPLITS = [DIFF_QK_W, DIFF_QK_W, DIFF_V_W, MLA_Q_RANK, MLA_KV_RANK, MLA_ROPE]

LANES = 128
NEG = -1e30
INT_MIN = -(2 ** 31)
VMEM_LIMIT = 48 * 1024 * 1024
FAR_WIDTH = 4
LOG2E = math.log2(math.e)


def _cparams(sem):
    return pltpu.CompilerParams(dimension_semantics=sem, vmem_limit_bytes=VMEM_LIMIT)


def _pick(n, prefs):
    for p in prefs:
        if n % p == 0:
            return p
    return n


def _resident(shape):
    return pl.BlockSpec(shape, lambda i: (0,) * len(shape), pipeline_mode=pl.Buffered(1))


def _mm_kernel(*refs, n_norm, cw, has_g, has_w2, has_rot, has_bias, has_res):
    it = iter(refs)
    x_ref = next(it)
    g_ref = next(it) if has_g else None
    w_ref = next(it)
    w2_ref = next(it) if has_w2 else None
    c1_ref = next(it) if has_rot else None
    c2_ref = next(it) if has_rot else None
    b_ref = next(it) if has_bias else None
    r_ref = next(it) if has_res else None
    o_ref = next(it)

    x = x_ref[...].astype(F32)
    if has_g:
        kb = x.shape[-1]
        if n_norm == kb:
            ms = jnp.mean(x * x, axis=-1, keepdims=True)
            x = x * lax.rsqrt(ms + RMS_EPS) * g_ref[...]
        else:
            normed = lax.broadcasted_iota(jnp.int32, x.shape, 1) < n_norm
            xs = jnp.where(normed, x, 0.0)
            ms = jnp.sum(xs * xs, axis=-1, keepdims=True) * (1.0 / n_norm)
            x = x * jnp.where(normed, lax.rsqrt(ms + RMS_EPS), 1.0) * g_ref[...]
    xn = x.astype(BF16)

    for c in range(o_ref.shape[-1] // cw):
        cs = slice(c * cw, (c + 1) * cw)
        acc = jnp.dot(xn, w_ref[:, cs], preferred_element_type=F32)
        if has_rot:
            r = c1_ref.shape[-1] // LANES
            ts = slice((c % r) * LANES, (c % r + 1) * LANES)
            acc = acc * jnp.concatenate([c1_ref[:, ts]] * (cw // LANES), axis=1)
            acc = acc + jnp.dot(xn, w2_ref[:, cs], preferred_element_type=F32) * jnp.concatenate(
                [c2_ref[:, ts]] * (cw // LANES), axis=1)
        if has_bias:
            acc = acc + b_ref[:, cs]
        if has_res:
            acc = acc + r_ref[:, cs]
        o_ref[:, cs] = acc.astype(o_ref.dtype)


def _mm(x, w, *, xblk=0, kb=None, g=None, n_norm=None, w2=None, rot=None, bias=None, res=None,
        out_dtype=F32):
    m = x.shape[0]
    kb = x.shape[1] if kb is None else kb
    n = w.shape[1]
    tm = _pick(m, (512, 256, 128))
    r = 1 if rot is None else rot[0].shape[1] // LANES
    cw = n // r if rot is not None else _pick(n, (1024, 768, 512, 384, 256, 128))
    args = [x]
    specs = [pl.BlockSpec((tm, kb), lambda i: (i, xblk))]
    if g is not None:
        n_norm = kb if n_norm is None else n_norm
        args.append(g.reshape(1, kb).astype(F32))
        specs.append(_resident((1, kb)))
    args.append(w)
    specs.append(_resident((kb, n)))
    if w2 is not None:
        args.append(w2)
        specs.append(_resident((kb, n)))
    if rot is not None:
        s_len = rot[0].shape[0]
        assert s_len % tm == 0
        for c in rot:
            args.append(c)
            specs.append(pl.BlockSpec((tm, r * LANES), lambda i: (i % (s_len // tm), 0)))
    if bias is not None:
        args.append(bias.reshape(1, n).astype(F32))
        specs.append(_resident((1, n)))
    if res is not None:
        args.append(res)
        specs.append(pl.BlockSpec((tm, n), lambda i: (i, 0)))
    return pl.pallas_call(
        functools.partial(_mm_kernel, n_norm=n_norm, cw=cw, has_g=g is not None, has_w2=w2 is not None,
                          has_rot=rot is not None, has_bias=bias is not None, has_res=res is not None),
        grid=(m // tm,),
        in_specs=specs,
        out_specs=pl.BlockSpec((tm, n), lambda i: (i, 0)),
        out_shape=jax.ShapeDtypeStruct((m, n), out_dtype),
        compiler_params=_cparams(("parallel",)),
        name="mm",
    )(*args)


def _ffn_kernel(*refs, cf, has_final):
    it = iter(refs)
    x_ref, g_ref, wg_ref, wu_ref, wd_ref = (next(it) for _ in range(5))
    fg_ref = next(it) if has_final else None
    o_ref = next(it)

    x = x_ref[...]
    ms = jnp.mean(x * x, axis=-1, keepdims=True)
    h = (x * lax.rsqrt(ms + RMS_EPS) * g_ref[...]).astype(BF16)
    y = x
    for c in range(wg_ref.shape[-1] // cf):
        cs = slice(c * cf, (c + 1) * cf)
        gate = jnp.dot(h, wg_ref[:, cs], preferred_element_type=F32)
        up = jnp.dot(h, wu_ref[:, cs], preferred_element_type=F32)
        act = gate * jax.nn.sigmoid(gate) * up
        y = y + jnp.dot(act.astype(BF16), wd_ref[cs, :], preferred_element_type=F32)
    if has_final:
        ms = jnp.mean(y * y, axis=-1, keepdims=True)
        y = y * lax.rsqrt(ms + RMS_EPS) * fg_ref[...]
    o_ref[...] = y


def _ffn(x, g, wg, wu, wd, final_g=None):
    m, d = x.shape
    hid = wg.shape[1]
    tm = _pick(m, (512, 256, 128))
    args = [x, g.reshape(1, d).astype(F32), wg, wu, wd]
    specs = [pl.BlockSpec((tm, d), lambda i: (i, 0)), _resident((1, d)),
             _resident((d, hid)), _resident((d, hid)), _resident((hid, d))]
    if final_g is not None:
        args.append(final_g.reshape(1, d).astype(F32))
        specs.append(_resident((1, d)))
    return pl.pallas_call(
        functools.partial(_ffn_kernel, cf=_pick(hid, (256, 128)), has_final=final_g is not None),
        grid=(m // tm,),
        in_specs=specs,
        out_specs=pl.BlockSpec((tm, d), lambda i: (i, 0)),
        out_shape=jax.ShapeDtypeStruct((m, d), F32),
        compiler_params=_cparams(("parallel",)),
        name="ffn",
    )(*args)


def _dsa_mask_kernel(qt_ref, ki_ref, w_ref, o_ref, keys_ref, *, tq, tk, nk, ksel, idx_scale):
    i = pl.program_id(1)
    sub = 8
    n_full = (i * tq) // tk
    off = i * tq - n_full * tk
    w = w_ref[0]
    kf = float(ksel)
    split = lambda a: a.reshape(tk // sub, sub, tq)

    def score_chunk(j, diag):
        kc = ki_ref[0, j]
        sc = jnp.zeros((tk, tq), F32)
        for h in range(IDX_HEADS):
            d = jnp.dot(kc, qt_ref[0, h], preferred_element_type=F32)
            sc = sc + jnp.maximum(d, 0.0) * w[h:h + 1, :]
        sc = sc * idx_scale
        bits = lax.bitcast_convert_type(sc, jnp.int32)
        key = jnp.where(bits < 0, INT_MIN - bits, bits)
        if diag:
            krow = lax.broadcasted_iota(jnp.int32, (tk, tq), 0)
            qcol = lax.broadcasted_iota(jnp.int32, (tk, tq), 1)
            key = jnp.where(krow <= qcol + off, key, INT_MIN)
        keys_ref[j] = key

    def _plain(j, c):
        score_chunk(j, False)
        return c

    lax.fori_loop(0, n_full, _plain, 0)
    score_chunk(n_full, True)

    def count(cand, strict):
        def body(j, acc):
            kc = split(keys_ref[j])
            hit = kc > cand[None] if strict else kc >= cand[None]
            return acc + jnp.sum(jnp.where(hit, 1.0, 0.0), axis=0)
        acc = lax.fori_loop(0, n_full + 1, body, jnp.zeros((sub, tq), F32))
        return jnp.broadcast_to(jnp.sum(acc, axis=0, keepdims=True), (sub, tq))

    zero = jnp.zeros((sub, tq), jnp.int32)
    prefix = jnp.where(count(zero, False) >= kf, 0, INT_MIN).astype(jnp.int32)

    def bit_body(b, prefix):
        cand = prefix + jnp.left_shift(jnp.int32(1), 30 - b)
        return jnp.where(count(cand, False) >= kf, cand, prefix)

    tau = lax.fori_loop(0, 31, bit_body, prefix)
    need = kf - count(tau, True)

    tri = (lax.broadcasted_iota(jnp.int32, (tk, tk), 1)
           <= lax.broadcasted_iota(jnp.int32, (tk, tk), 0)).astype(BF16)
    ones_s = jnp.ones((sub, tk), BF16)

    def emit(j, carry):
        kc = split(keys_ref[j])
        eq = kc == tau[None]
        eqf = jnp.where(eq, 1.0, 0.0).reshape(tk, tq).astype(BF16)
        rank = split(jnp.dot(tri, eqf, preferred_element_type=F32)) + carry[None]
        sel = (kc > tau[None]) | (eq & (rank <= need[None]) & (kc > INT_MIN))
        o_ref[0, j] = jnp.where(sel, 0.0, NEG).reshape(tk, tq).T.astype(o_ref.dtype)
        return carry + jnp.dot(ones_s, eqf, preferred_element_type=F32)

    lax.fori_loop(0, n_full + 1, emit, jnp.zeros((sub, tq), F32))

    def _fill(j, c):
        o_ref[0, j] = jnp.full((tq, tk), NEG, o_ref.dtype)
        return c

    lax.fori_loop(n_full + 1, nk, _fill, 0)


def _dsa_mask(iq, ik, iw, tk, ksel):
    bsz, s_len, hi, di = iq.shape
    nk = s_len // tk
    tq = tk
    q_hi, q_lo = _hi_lo(iq)
    k_hi, k_lo = _hi_lo(ik)
    qt = jnp.transpose(jnp.concatenate([q_hi, q_hi, q_lo], axis=-1), (0, 2, 3, 1))
    ki = jnp.concatenate([k_hi, k_lo, k_hi], axis=-1).reshape(bsz, nk, tk, 3 * di)
    wt = jnp.transpose(iw, (0, 2, 1))
    idx_scale = (di ** -0.5) * (hi ** -0.5)
    return pl.pallas_call(
        functools.partial(_dsa_mask_kernel, tq=tq, tk=tk, nk=nk, ksel=ksel, idx_scale=idx_scale),
        grid=(bsz, s_len // tq),
        in_specs=[pl.BlockSpec((1, hi, 3 * di, tq), lambda b, i: (b, 0, 0, i)),
                  pl.BlockSpec((1, nk, tk, 3 * di), lambda b, i: (b, 0, 0, 0)),
                  pl.BlockSpec((1, hi, tq), lambda b, i: (b, 0, i))],
        out_specs=pl.BlockSpec((1, nk, tq, tk), lambda b, i: (b, 0, i, 0)),
        out_shape=jax.ShapeDtypeStruct((bsz, nk, s_len, tk), BF16),
        scratch_shapes=[pltpu.VMEM((nk, tk, tq), jnp.int32)],
        compiler_params=_cparams(("parallel", "arbitrary")),
        name="dsa_mask",
    )(qt, ki, wt)


def _finish_tail(s, band, causal):
    nr, nc = s.shape[0] // LANES, s.shape[1] // LANES
    keep = (lax.broadcasted_iota(jnp.int32, (LANES, LANES), 1)
            <= lax.broadcasted_iota(jnp.int32, (LANES, LANES), 0))
    rows = []
    for a in range(nr):
        blocks = []
        for b in range(nc):
            d = (nc - nr) + a - b
            blk = s[a * LANES:(a + 1) * LANES, b * LANES:(b + 1) * LANES]
            if band is not None and d in (0, 1):
                blk = blk + band[d]
            if causal and d == 0:
                blk = jnp.where(keep, blk, NEG)
            elif causal and d < 0:
                blk = jnp.full_like(blk, NEG)
            blocks.append(blk)
        rows.append(jnp.concatenate(blocks, axis=1))
    return jnp.concatenate(rows, axis=0)


def _flash_kernel(*refs, tile, wq, dv, v_shared, q_axis, has_kbias, has_band, has_mask, diff):
    it = iter(refs)
    q_ref, k_ref, v_ref = next(it), next(it), next(it)
    kb_ref = next(it) if has_kbias else None
    t_ref = next(it) if has_band else None
    mk_ref = next(it) if has_mask else None
    lam_ref = next(it) if diff else None
    sg_ref = next(it) if diff else None
    o_ref, m_ref, acc_ref = next(it), next(it), next(it)
    l_ref = next(it) if v_shared else None

    i = pl.program_id(q_axis)
    qblk = q_ref[0]
    if wq == LANES:
        lane = lax.broadcasted_iota(jnp.int32, qblk.shape, 1)
        q32 = qblk.astype(F32)
        qs = [jnp.where(lane < LANES // 2, q32, 0.0).astype(BF16),
              jnp.where(lane >= LANES // 2, q32, 0.0).astype(BF16)]
    else:
        qs = [qblk[:, :LANES], qblk[:, LANES:]]
    m_ref[...] = jnp.full_like(m_ref, NEG)
    acc_ref[...] = jnp.zeros_like(acc_ref)
    if v_shared:
        l_ref[...] = jnp.zeros_like(l_ref)

    def step(j, width=1, tail=False):
        cols = width * tile
        off = pl.multiple_of(j * tile, tile)
        kc = k_ref[0, pl.ds(off, cols), :]
        vc = v_ref[0, pl.ds(off, cols), :]
        cat = lambda f: f(j) if width == 1 else jnp.concatenate([f(j + t) for t in range(width)], axis=1)
        mk = cat(lambda t: mk_ref[0, t]).astype(F32) if has_mask else None
        for a in range(2):
            ka = kc if wq == LANES else kc[:, a * LANES:(a + 1) * LANES]
            s = lax.dot_general(qs[a], ka, (((1,), (1,)), ((), ())), preferred_element_type=F32)
            if has_kbias:
                s = s + cat(lambda t: kb_ref[0, a, t])
            if tail:
                s = _finish_tail(s, (t_ref[a, 0], t_ref[a, 1]) if has_band else None, not has_mask)
            if has_mask:
                s = s + mk
            m_old = m_ref[a]
            m_new = jnp.maximum(m_old, jnp.max(s, axis=-1, keepdims=True))
            p = jnp.exp2(s - jnp.concatenate([m_new] * (cols // LANES), axis=1))
            alpha = jnp.exp2(m_old - m_new)
            va = vc if v_shared else vc[:, a * LANES:(a + 1) * LANES]
            if v_shared:
                l_ref[a] = l_ref[a] * alpha + jnp.sum(p, axis=-1, keepdims=True)
            acc_ref[a] = acc_ref[a] * alpha + jnp.dot(p.astype(BF16), va, preferred_element_type=F32)
            m_ref[a] = m_new

    n_far = jnp.maximum(i - 1, 0)

    def _far(t, c):
        step(t * FAR_WIDTH, width=FAR_WIDTH)
        return c

    lax.fori_loop(0, n_far // FAR_WIDTH, _far, 0)
    w = FAR_WIDTH // 2
    while w >= 1:
        @pl.when(n_far % (2 * w) >= w)
        def _(w=w):
            step((n_far // (2 * w)) * (2 * w), width=w)
        w //= 2

    @pl.when(i >= 1)
    def _():
        step(i - 1, width=2, tail=True)

    @pl.when(i == 0)
    def _():
        step(0, tail=True)

    outs = []
    for a in range(2):
        acc = acc_ref[a]
        outs.append(acc / l_ref[a] if v_shared else acc[:, :dv] / acc[:, dv:dv + 1])
    if diff:
        d = outs[0] - lam_ref[...] * outs[1]
        ms = jnp.mean(d * d, axis=-1, keepdims=True)
        o_ref[0] = (d * lax.rsqrt(ms + RMS_EPS) * sg_ref[...]).astype(o_ref.dtype)
    else:
        o_ref[0] = jnp.concatenate(outs, axis=1).astype(o_ref.dtype)


def _flash(qa, ka, va, *, n_pairs, qoff, koff, voff, wq, dv, tile, v_shared=False, kbias=None,
           band=None, mask=None, diff=None):
    bsz, s_len, _ = qa.shape
    nk = s_len // tile
    vw = LANES if v_shared else 2 * LANES
    assert qoff % wq == 0 and koff % wq == 0 and voff % vw == 0 and (not v_shared or dv == LANES)
    qb, kb_, vb = qoff // wq, koff // wq, voff // vw
    if mask is None:
        grid, q_axis = (bsz, n_pairs, nk), 2
        ix = lambda f: (lambda b, h, i: f(b, h, i))
    else:
        grid, q_axis = (bsz, nk, n_pairs), 1
        ix = lambda f: (lambda b, i, h: f(b, h, i))
    args = [qa, ka, va]
    specs = [pl.BlockSpec((1, tile, wq), ix(lambda b, h, i: (b, i, qb + h))),
             pl.BlockSpec((1, s_len, wq), ix(lambda b, h, i: (b, 0, kb_ + h))),
             pl.BlockSpec((1, s_len, vw), ix(lambda b, h, i: (b, 0, vb + h)))]
    if kbias is not None:
        args.append(kbias)
        specs.append(pl.BlockSpec((1, 2, nk, 1, tile), ix(lambda b, h, i: (b, h, 0, 0, 0))))
    if band is not None:
        args.append(band)
        specs.append(pl.BlockSpec((2, 2, LANES, LANES), ix(lambda b, h, i: (h, 0, 0, 0))))
    if mask is not None:
        args.append(mask)
        specs.append(pl.BlockSpec((1, nk, tile, tile), ix(lambda b, h, i: (b, 0, i, 0))))
    if diff is not None:
        for t in diff:
            args.append(t.reshape(1, LANES).astype(F32))
            specs.append(pl.BlockSpec((1, LANES), ix(lambda b, h, i: (0, 0))))
    return pl.pallas_call(
        functools.partial(_flash_kernel, tile=tile, wq=wq, dv=dv, v_shared=v_shared, q_axis=q_axis,
                          has_kbias=kbias is not None, has_band=band is not None,
                          has_mask=mask is not None, diff=diff is not None),
        grid=grid,
        in_specs=specs,
        out_specs=pl.BlockSpec((1, tile, LANES), ix(lambda b, h, i: (b, i, h))),
        out_shape=jax.ShapeDtypeStruct((bsz, s_len, n_pairs * LANES), BF16),
        scratch_shapes=[pltpu.VMEM((2, tile, LANES), F32)] * (3 if v_shared else 2),
        compiler_params=_cparams(("parallel", "parallel", "arbitrary")),
        name="flash",
    )(*args)


def _t5_bucket(dist):
    exact = T5_BUCKETS // 2
    d = np.maximum(dist, 1).astype(np.float32)
    log_b = exact + (np.log(d / np.float32(exact)) / np.float32(math.log(T5_MAX_DIST / exact))
                     * np.float32(T5_BUCKETS - exact)).astype(np.int32)
    log_b = np.minimum(log_b, T5_BUCKETS - 1)
    return np.where(dist < exact, dist, log_b).astype(np.int32)


def _band_tiles(t5_table):
    assert T5_MAX_DIST <= LANES
    r = np.arange(LANES)[:, None]
    c = np.arange(LANES)[None, :]
    bmap = np.stack([_t5_bucket(np.maximum(r - c, 0)), _t5_bucket(LANES + r - c)])
    t = t5_table.astype(F32)
    return jnp.transpose(t[bmap] - t[T5_BUCKETS - 1], (3, 0, 1, 2)) * LOG2E


def _with_ones_col(w, heads, d):
    k = w.shape[0]
    dvp = -(-(d + 1) // LANES) * LANES
    wp = jnp.pad(w.reshape(k, heads, d), ((0, 0), (0, 0), (0, dvp - d))).reshape(k, heads * dvp)
    bias = np.zeros((heads, dvp), np.float32)
    bias[:, d] = 1.0
    return wp, jnp.asarray(bias.reshape(-1))


def _hi_lo(a):
    hi = a.astype(BF16)
    lo = (a - hi.astype(F32)).astype(BF16)
    return hi, lo


def _pad_to(a, n, axis):
    pad = [(0, 0)] * a.ndim
    pad[axis] = (0, n - a.shape[axis])
    return jnp.pad(a, pad)


def _rope_tables(s_len):
    pos = jnp.arange(s_len, dtype=F32)
    inv = ROPE_THETA ** (-jnp.arange(0, MLA_ROPE, 2, dtype=F32) / MLA_ROPE)
    ang = pos[:, None] * inv[None, :]
    ang = jnp.concatenate([ang, ang], axis=-1)
    return jnp.cos(ang), jnp.sin(ang)


def _rot_half_matrix():
    half = MLA_ROPE // 2
    r = np.zeros((MLA_ROPE, MLA_ROPE), np.float32)
    for c in range(half):
        r[c + half, c] = -1.0
        r[c, c + half] = 1.0
    return jnp.asarray(r)


def _even_mixer(x2, bsz, s_len, g, w_in, b_forget, w_out, t5_table, tile):
    m, dm = x2.shape
    nk = s_len // tile
    scale = HEAD_DIM ** -0.5 * LOG2E
    wfq, wfk, wfv, wff, wdq, wdk, wdv, wiq, wik, wiw = jnp.split(
        w_in, [int(o) for o in np.cumsum(EVEN_SPLITS)[:-1]], axis=1)
    wfv1, ones_f = _with_ones_col(wfv, FOX_HEADS, HEAD_DIM)
    wdv1, ones_d = _with_ones_col(wdv, DSA_HEADS, HEAD_DIM)
    w_main = jnp.concatenate([wfq * scale, wfk, wfv1, wdq * scale, wdk, wdv1], axis=1).astype(BF16)
    zeros = lambda n: jnp.zeros((n,), F32)
    b_main = jnp.concatenate([zeros(2 * FOX_W), ones_f, zeros(2 * DSA_W), ones_d])
    qoff_f, koff_f, voff_f = 0, FOX_W, 2 * FOX_W
    qoff_d = voff_f + wfv1.shape[1]
    koff_d, voff_d = qoff_d + DSA_W, qoff_d + 2 * DSA_W
    main = _mm(x2, w_main, g=g, bias=b_main, out_dtype=BF16).reshape(bsz, s_len, -1)
    w_aux = jnp.concatenate([_pad_to(jnp.concatenate([wff, wiw], axis=1), LANES, 1),
                             _pad_to(wik, LANES, 1), wiq], axis=1).astype(BF16)
    aux = _mm(x2, w_aux, g=g).reshape(bsz, s_len, -1)
    ff, iw = aux[..., :FOX_HEADS], aux[..., FOX_HEADS:FOX_HEADS + IDX_HEADS]
    ik = aux[..., LANES:LANES + IDX_DIM]
    iq = aux[..., 2 * LANES:2 * LANES + IDX_HEADS * IDX_DIM]

    log_f = jax.nn.log_sigmoid(ff + b_forget.astype(F32))
    log_cum = jnp.cumsum(log_f, axis=1)
    kbias = (-LOG2E * jnp.transpose(log_cum, (0, 2, 1))).reshape(bsz, FOX_HEADS, nk, 1, tile)
    fox = _flash(main, main, main, n_pairs=FOX_HEADS // 2, qoff=qoff_f, koff=koff_f, voff=voff_f,
                 wq=LANES, dv=HEAD_DIM, tile=tile, kbias=kbias)

    mask = _dsa_mask(iq.reshape(bsz, s_len, IDX_HEADS, IDX_DIM), ik, iw, tile, min(DSA_TOPK, s_len // 4))
    dsa = _flash(main, main, main, n_pairs=DSA_HEADS // 2, qoff=qoff_d, koff=koff_d, voff=voff_d,
                 wq=LANES, dv=HEAD_DIM, tile=tile, band=_band_tiles(t5_table), mask=mask)

    mixed = jnp.concatenate([fox, dsa], axis=-1).reshape(m, -1)
    return _mm(mixed, w_out.astype(BF16), res=x2)


def _odd_mixer(x2, bsz, s_len, g, w_in, lq1, lk1, lq2, lk2, subln_g, q_norm_g, w_uq, kv_norm_g, w_ukv,
               w_out, t5_table, lambda_init, tile):
    m, dm = x2.shape
    wcq, wck, wcv, wmq, wmkv, wmkr = jnp.split(
        w_in, [int(o) for o in np.cumsum(ODD_SPLITS)[:-1]], axis=1)
    w_main = jnp.concatenate([wcq * (HEAD_DIM ** -0.5 * LOG2E), wck, wcv], axis=1).astype(BF16)
    main = _mm(x2, w_main, g=g, out_dtype=BF16).reshape(bsz, s_len, -1)
    lat_w = MLA_Q_RANK + 2 * LANES
    w_aux = _pad_to(jnp.concatenate([wmq, wmkv, wmkr], axis=1), lat_w, 1).astype(BF16)
    aux = _mm(x2, w_aux, g=g)

    lam = (jnp.exp(jnp.sum(lq1.astype(F32) * lk1.astype(F32)))
           - jnp.exp(jnp.sum(lq2.astype(F32) * lk2.astype(F32))) + lambda_init)
    diff = _flash(main, main, main, n_pairs=DIFF_HEADS, qoff=0, koff=DIFF_QK_W, voff=2 * DIFF_QK_W,
                  wq=LANES, dv=DIFF_VDIM, tile=tile, v_shared=True, band=_band_tiles(t5_table),
                  diff=(jnp.full((LANES,), lam, F32), subln_g.astype(F32) * (1.0 - lambda_init)))

    dqk = MLA_NOPE + MLA_ROPE
    cos, sin = _rope_tables(s_len)
    rmat = _rot_half_matrix()
    head_tab = lambda first, rope: jnp.concatenate(
        [jnp.full((s_len, MLA_NOPE), first, F32), rope, jnp.zeros((s_len, LANES - dqk), F32)], axis=1)
    c1, c2 = head_tab(1.0, cos), head_tab(0.0, sin)
    wq3 = w_uq.reshape(MLA_Q_RANK, MLA_HEADS, dqk)
    wq1 = _pad_to(wq3, LANES, 2).reshape(MLA_Q_RANK, -1)
    wq2 = jnp.pad(jnp.einsum('khr,rs->khs', wq3[..., MLA_NOPE:], rmat),
                  ((0, 0), (0, 0), (MLA_NOPE, LANES - dqk))).reshape(MLA_Q_RANK, -1)
    qscale = dqk ** -0.5 * LOG2E
    mla_q = _mm(aux, wq1.astype(BF16), xblk=0, kb=MLA_Q_RANK, g=q_norm_g, w2=wq2.astype(BF16),
                rot=(c1 * qscale, c2 * qscale), out_dtype=BF16).reshape(bsz, s_len, -1)
    wkv3 = w_ukv.reshape(MLA_KV_RANK, MLA_HEADS, MLA_NOPE + MLA_VDIM)
    wk_nope = _pad_to(wkv3[..., :MLA_NOPE], LANES, 2).reshape(MLA_KV_RANK, -1)
    wv1, ones_v = _with_ones_col(wkv3[..., MLA_NOPE:].reshape(MLA_KV_RANK, -1), MLA_HEADS, MLA_VDIM)
    eye_blk = lambda mat: jnp.tile(jnp.pad(mat, ((0, 0), (MLA_NOPE, LANES - dqk))), (1, MLA_HEADS))
    kw = MLA_HEADS * LANES
    top = jnp.concatenate([wk_nope, wv1], axis=1)
    mid1 = jnp.concatenate([eye_blk(jnp.eye(MLA_ROPE, dtype=F32)), jnp.zeros((MLA_ROPE, kw), F32)], axis=1)
    mid2 = jnp.concatenate([eye_blk(rmat), jnp.zeros((MLA_ROPE, kw), F32)], axis=1)
    wk1 = _pad_to(jnp.concatenate([top, mid1], axis=0), 2 * LANES, 0)
    wk2 = _pad_to(jnp.concatenate([jnp.zeros_like(top), mid2], axis=0), 2 * LANES, 0)
    ones_tab = jnp.ones((s_len, LANES), F32)
    g_kv = jnp.concatenate([kv_norm_g.astype(F32), jnp.ones((2 * LANES - MLA_KV_RANK,), F32)])
    mla_kv = _mm(aux, wk1.astype(BF16), xblk=MLA_Q_RANK // (2 * LANES), kb=2 * LANES, g=g_kv,
                 n_norm=MLA_KV_RANK, w2=wk2.astype(BF16),
                 rot=(jnp.concatenate([c1, ones_tab], axis=1), jnp.concatenate([c2, 0.0 * ones_tab], axis=1)),
                 bias=jnp.concatenate([jnp.zeros((kw,), F32), ones_v]),
                 out_dtype=BF16).reshape(bsz, s_len, -1)
    mla = _flash(mla_q, mla_kv, mla_kv, n_pairs=MLA_HEADS // 2, qoff=0, koff=0, voff=kw,
                 wq=2 * LANES, dv=MLA_VDIM, tile=tile)

    mixed = jnp.concatenate([diff, mla], axis=-1).reshape(m, -1)
    return _mm(mixed, w_out.astype(BF16), res=x2)


def kernel(x, norm_mix_g, norm_ffn_g, w_in_even, b_forget, w_out_even, w_in_odd, lambda_q1, lambda_k1, lambda_q2, lambda_k2, diff_subln_g, mla_q_norm_g, w_mla_uq, mla_kv_norm_g, w_mla_ukv, w_out_odd, t5_bias, w_ffn_gate, w_ffn_up, w_ffn_down, final_norm_g):
    bsz, s_len, dm = x.shape
    depth = norm_mix_g.shape[0]
    tile = _pick(s_len, (512, 256, 128))
    x2 = x.astype(F32).reshape(bsz * s_len, dm)
    for layer in range(depth):
        j = layer // 2
        if layer % 2 == 0:
            x2 = _even_mixer(x2, bsz, s_len, norm_mix_g[layer], w_in_even[j], b_forget[j], w_out_even[j],
                             t5_bias, tile)
        else:
            lambda_init = 0.8 - 0.6 * math.exp(-0.3 * layer)
            x2 = _odd_mixer(x2, bsz, s_len, norm_mix_g[layer], w_in_odd[j], lambda_q1[j], lambda_k1[j],
                            lambda_q2[j], lambda_k2[j], diff_subln_g[j], mla_q_norm_g[j], w_mla_uq[j],
                            mla_kv_norm_g[j], w_mla_ukv[j], w_out_odd[j], t5_bias, lambda_init, tile)
        last = layer == depth - 1
        x2 = _ffn(x2, norm_ffn_g[layer], w_ffn_gate[layer].astype(BF16), w_ffn_up[layer].astype(BF16),
                  w_ffn_down[layer].astype(BF16), final_g=final_norm_g if last else None)
    return x2.reshape(bsz, s_len, dm)
```

```python
import functools
import math

import numpy as np
import jax
import jax.numpy as jnp
from jax import lax
from jax.experimental import pallas as pl
from jax.experimental.pallas import tpu as pltpu

F32 = jnp.float32
BF16 = jnp.bfloat16

HEAD_DIM = 64
RMS_EPS = 1e-6
FOX_HEADS = 8
DSA_HEADS = 8
IDX_HEADS = 4
IDX_DIM = 64
DSA_TOPK = 256
DIFF_HEADS = 4
DIFF_VDIM = 2 * HEAD_DIM
MLA_HEADS = 8
MLA_NOPE = 64
MLA_ROPE = 32
MLA_VDIM = 64
MLA_Q_RANK = 256
MLA_KV_RANK = 128
ROPE_THETA = 10000.0
T5_BUCKETS = 32
T5_MAX_DIST = 128

FOX_W = FOX_HEADS * HEAD_DIM
DSA_W = DSA_HEADS * HEAD_DIM
EVEN_SPLITS = [FOX_W, FOX_W, FOX_W, FOX_HEADS, DSA_W, DSA_W, DSA_W,
               IDX_HEADS * IDX_DIM, IDX_DIM, IDX_HEADS]
DIFF_QK_W = DIFF_HEADS * 2 * HEAD_DIM
DIFF_V_W = DIFF_HEADS * DIFF_VDIM
ODD_SPLITS = [DIFF_QK_W, DIFF_QK_W, DIFF_V_W, MLA_Q_RANK, MLA_KV_RANK, MLA_ROPE]

LANES = 128
NEG = -1e30
INT_MIN = -(2 ** 31)
VMEM_LIMIT = 48 * 1024 * 1024
FAR_WIDTH = 4
UNDERFLOW_LOG2 = 160.0
LOG2E = math.log2(math.e)


def _cparams(sem):
    return pltpu.CompilerParams(dimension_semantics=sem, vmem_limit_bytes=VMEM_LIMIT)


def _pick(n, prefs):
    for p in prefs:
        if n % p == 0:
            return p
    return n


def _resident(shape):
    return pl.BlockSpec(shape, lambda i: (0,) * len(shape), pipeline_mode=pl.Buffered(1))


def _mm_kernel(*refs, n_norm, cw, has_g, has_w2, has_rot, has_bias, has_res):
    it = iter(refs)
    x_ref = next(it)
    g_ref = next(it) if has_g else None
    w_ref = next(it)
    w2_ref = next(it) if has_w2 else None
    c1_ref = next(it) if has_rot else None
    c2_ref = next(it) if has_rot else None
    b_ref = next(it) if has_bias else None
    r_ref = next(it) if has_res else None
    o_ref = next(it)

    x = x_ref[...].astype(F32)
    if has_g:
        kb = x.shape[-1]
        if n_norm == kb:
            ms = jnp.mean(x * x, axis=-1, keepdims=True)
            x = x * lax.rsqrt(ms + RMS_EPS) * g_ref[...]
        else:
            normed = lax.broadcasted_iota(jnp.int32, x.shape, 1) < n_norm
            xs = jnp.where(normed, x, 0.0)
            ms = jnp.sum(xs * xs, axis=-1, keepdims=True) * (1.0 / n_norm)
            x = x * jnp.where(normed, lax.rsqrt(ms + RMS_EPS), 1.0) * g_ref[...]
    xn = x.astype(BF16)

    for c in range(o_ref.shape[-1] // cw):
        cs = slice(c * cw, (c + 1) * cw)
        acc = jnp.dot(xn, w_ref[:, cs], preferred_element_type=F32)
        if has_rot:
            r = c1_ref.shape[-1] // LANES
            ts = slice((c % r) * LANES, (c % r + 1) * LANES)
            acc = acc * jnp.concatenate([c1_ref[:, ts]] * (cw // LANES), axis=1)
            acc = acc + jnp.dot(xn, w2_ref[:, cs], preferred_element_type=F32) * jnp.concatenate(
                [c2_ref[:, ts]] * (cw // LANES), axis=1)
        if has_bias:
            acc = acc + b_ref[:, cs]
        if has_res:
            acc = acc + r_ref[:, cs]
        o_ref[:, cs] = acc.astype(o_ref.dtype)


def _mm(x, w, *, xblk=0, kb=None, g=None, n_norm=None, w2=None, rot=None, bias=None, res=None,
        out_dtype=F32):
    m = x.shape[0]
    kb = x.shape[1] if kb is None else kb
    n = w.shape[1]
    tm = _pick(m, (512, 256, 128))
    r = 1 if rot is None else rot[0].shape[1] // LANES
    cw = n // r if rot is not None else _pick(n, (1024, 768, 512, 384, 256, 128))
    args = [x]
    specs = [pl.BlockSpec((tm, kb), lambda i: (i, xblk))]
    if g is not None:
        n_norm = kb if n_norm is None else n_norm
        args.append(g.reshape(1, kb).astype(F32))
        specs.append(_resident((1, kb)))
    args.append(w)
    specs.append(_resident((kb, n)))
    if w2 is not None:
        args.append(w2)
        specs.append(_resident((kb, n)))
    if rot is not None:
        s_len = rot[0].shape[0]
        assert s_len % tm == 0
        for c in rot:
            args.append(c)
            specs.append(pl.BlockSpec((tm, r * LANES), lambda i: (i % (s_len // tm), 0)))
    if bias is not None:
        args.append(bias.reshape(1, n).astype(F32))
        specs.append(_resident((1, n)))
    if res is not None:
        args.append(res)
        specs.append(pl.BlockSpec((tm, n), lambda i: (i, 0)))
    return pl.pallas_call(
        functools.partial(_mm_kernel, n_norm=n_norm, cw=cw, has_g=g is not None, has_w2=w2 is not None,
                          has_rot=rot is not None, has_bias=bias is not None, has_res=res is not None),
        grid=(m // tm,),
        in_specs=specs,
        out_specs=pl.BlockSpec((tm, n), lambda i: (i, 0)),
        out_shape=jax.ShapeDtypeStruct((m, n), out_dtype),
        compiler_params=_cparams(("parallel",)),
        name="mm",
    )(*args)


def _ffn_kernel(*refs, cf, has_final):
    it = iter(refs)
    x_ref, g_ref, wg_ref, wu_ref, wd_ref = (next(it) for _ in range(5))
    fg_ref = next(it) if has_final else None
    o_ref = next(it)

    x = x_ref[...]
    ms = jnp.mean(x * x, axis=-1, keepdims=True)
    h = (x * lax.rsqrt(ms + RMS_EPS) * g_ref[...]).astype(BF16)
    y = x
    for c in range(wg_ref.shape[-1] // cf):
        cs = slice(c * cf, (c + 1) * cf)
        gate = jnp.dot(h, wg_ref[:, cs], preferred_element_type=F32)
        up = jnp.dot(h, wu_ref[:, cs], preferred_element_type=F32)
        act = gate * jax.nn.sigmoid(gate) * up
        y = y + jnp.dot(act.astype(BF16), wd_ref[cs, :], preferred_element_type=F32)
    if has_final:
        ms = jnp.mean(y * y, axis=-1, keepdims=True)
        y = y * lax.rsqrt(ms + RMS_EPS) * fg_ref[...]
    o_ref[...] = y


def _ffn(x, g, wg, wu, wd, final_g=None):
    m, d = x.shape
    hid = wg.shape[1]
    tm = _pick(m, (512, 256, 128))
    args = [x, g.reshape(1, d).astype(F32), wg, wu, wd]
    specs = [pl.BlockSpec((tm, d), lambda i: (i, 0)), _resident((1, d)),
             _resident((d, hid)), _resident((d, hid)), _resident((hid, d))]
    if final_g is not None:
        args.append(final_g.reshape(1, d).astype(F32))
        specs.append(_resident((1, d)))
    return pl.pallas_call(
        functools.partial(_ffn_kernel, cf=_pick(hid, (256, 128)), has_final=final_g is not None),
        grid=(m // tm,),
        in_specs=specs,
        out_specs=pl.BlockSpec((tm, d), lambda i: (i, 0)),
        out_shape=jax.ShapeDtypeStruct((m, d), F32),
        compiler_params=_cparams(("parallel",)),
        name="ffn",
    )(*args)


def _dsa_mask_kernel(qt_ref, ki_ref, w_ref, o_ref, keys_ref, *, tq, tk, nk, ksel, idx_scale):
    i = pl.program_id(1)
    sub = 8
    n_full = (i * tq) // tk
    off = i * tq - n_full * tk
    w = w_ref[0]
    kf = float(ksel)
    split = lambda a: a.reshape(tk // sub, sub, tq)

    def score_chunk(j, diag):
        kc = ki_ref[0, j]
        sc = jnp.zeros((tk, tq), F32)
        for h in range(IDX_HEADS):
            d = jnp.dot(kc, qt_ref[0, h], preferred_element_type=F32)
            sc = sc + jnp.maximum(d, 0.0) * w[h:h + 1, :]
        sc = sc * idx_scale
        bits = lax.bitcast_convert_type(sc, jnp.int32)
        key = jnp.where(bits < 0, INT_MIN - bits, bits)
        if diag:
            krow = lax.broadcasted_iota(jnp.int32, (tk, tq), 0)
            qcol = lax.broadcasted_iota(jnp.int32, (tk, tq), 1)
            key = jnp.where(krow <= qcol + off, key, INT_MIN)
        keys_ref[j] = key

    def _plain(j, c):
        score_chunk(j, False)
        return c

    lax.fori_loop(0, n_full, _plain, 0)
    score_chunk(n_full, True)

    def count(cand, strict):
        def body(j, acc):
            kc = split(keys_ref[j])
            hit = kc > cand[None] if strict else kc >= cand[None]
            return acc + jnp.sum(jnp.where(hit, 1.0, 0.0), axis=0)
        acc = lax.fori_loop(0, n_full + 1, body, jnp.zeros((sub, tq), F32))
        return jnp.broadcast_to(jnp.sum(acc, axis=0, keepdims=True), (sub, tq))

    zero = jnp.zeros((sub, tq), jnp.int32)
    prefix = jnp.where(count(zero, False) >= kf, 0, INT_MIN).astype(jnp.int32)

    def bit_body(b, prefix):
        cand = prefix + jnp.left_shift(jnp.int32(1), 30 - b)
        return jnp.where(count(cand, False) >= kf, cand, prefix)

    tau = lax.fori_loop(0, 31, bit_body, prefix)
    need = kf - count(tau, True)

    tri = (lax.broadcasted_iota(jnp.int32, (tk, tk), 1)
           <= lax.broadcasted_iota(jnp.int32, (tk, tk), 0)).astype(BF16)
    ones_s = jnp.ones((sub, tk), BF16)

    def emit(j, carry):
        kc = split(keys_ref[j])
        eq = kc == tau[None]
        eqf = jnp.where(eq, 1.0, 0.0).reshape(tk, tq).astype(BF16)
        rank = split(jnp.dot(tri, eqf, preferred_element_type=F32)) + carry[None]
        sel = (kc > tau[None]) | (eq & (rank <= need[None]) & (kc > INT_MIN))
        o_ref[0, j] = jnp.where(sel, 0.0, NEG).reshape(tk, tq).T.astype(o_ref.dtype)
        return carry + jnp.dot(ones_s, eqf, preferred_element_type=F32)

    lax.fori_loop(0, n_full + 1, emit, jnp.zeros((sub, tq), F32))

    def _fill(j, c):
        o_ref[0, j] = jnp.full((tq, tk), NEG, o_ref.dtype)
        return c

    lax.fori_loop(n_full + 1, nk, _fill, 0)


def _dsa_mask(iq, ik, iw, tk, ksel):
    bsz, s_len, hi, di = iq.shape
    nk = s_len // tk
    tq = tk
    q_hi, q_lo = _hi_lo(iq)
    k_hi, k_lo = _hi_lo(ik)
    qt = jnp.transpose(jnp.concatenate([q_hi, q_hi, q_lo], axis=-1), (0, 2, 3, 1))
    ki = jnp.concatenate([k_hi, k_lo, k_hi], axis=-1).reshape(bsz, nk, tk, 3 * di)
    wt = jnp.transpose(iw, (0, 2, 1))
    idx_scale = (di ** -0.5) * (hi ** -0.5)
    return pl.pallas_call(
        functools.partial(_dsa_mask_kernel, tq=tq, tk=tk, nk=nk, ksel=ksel, idx_scale=idx_scale),
        grid=(bsz, s_len // tq),
        in_specs=[pl.BlockSpec((1, hi, 3 * di, tq), lambda b, i: (b, 0, 0, i)),
                  pl.BlockSpec((1, nk, tk, 3 * di), lambda b, i: (b, 0, 0, 0)),
                  pl.BlockSpec((1, hi, tq), lambda b, i: (b, 0, i))],
        out_specs=pl.BlockSpec((1, nk, tq, tk), lambda b, i: (b, 0, i, 0)),
        out_shape=jax.ShapeDtypeStruct((bsz, nk, s_len, tk), BF16),
        scratch_shapes=[pltpu.VMEM((nk, tk, tq), jnp.int32)],
        compiler_params=_cparams(("parallel", "arbitrary")),
        name="dsa_mask",
    )(qt, ki, wt)


def _finish_tail(s, band, causal):
    nr, nc = s.shape[0] // LANES, s.shape[1] // LANES
    keep = (lax.broadcasted_iota(jnp.int32, (LANES, LANES), 1)
            <= lax.broadcasted_iota(jnp.int32, (LANES, LANES), 0))
    rows = []
    for a in range(nr):
        blocks = []
        for b in range(nc):
            d = (nc - nr) + a - b
            blk = s[a * LANES:(a + 1) * LANES, b * LANES:(b + 1) * LANES]
            if band is not None and d in (0, 1):
                blk = blk + band[d]
            if causal and d == 0:
                blk = jnp.where(keep, blk, NEG)
            elif causal and d < 0:
                blk = jnp.full_like(blk, NEG)
            blocks.append(blk)
        rows.append(jnp.concatenate(blocks, axis=1))
    return jnp.concatenate(rows, axis=0)


def _flash_kernel(*refs, tile, wq, dv, v_shared, q_axis, has_kbias, has_band, has_mask, has_skip, diff):
    it = iter(refs)
    q_ref, k_ref, v_ref = next(it), next(it), next(it)
    kb_ref = next(it) if has_kbias else None
    t_ref = next(it) if has_band else None
    mk_ref = next(it) if has_mask else None
    lam_ref = next(it) if diff else None
    sg_ref = next(it) if diff else None
    skip_ref = next(it) if has_skip else None
    o_ref, m_ref, acc_ref = next(it), next(it), next(it)
    l_ref = next(it) if v_shared else None

    i = pl.program_id(q_axis)
    qblk = q_ref[0]
    if wq == LANES:
        lane = lax.broadcasted_iota(jnp.int32, qblk.shape, 1)
        q32 = qblk.astype(F32)
        qs = [jnp.where(lane < LANES // 2, q32, 0.0).astype(BF16),
              jnp.where(lane >= LANES // 2, q32, 0.0).astype(BF16)]
    else:
        qs = [qblk[:, :LANES], qblk[:, LANES:]]
    m_ref[...] = jnp.full_like(m_ref, NEG)
    acc_ref[...] = jnp.zeros_like(acc_ref)
    if v_shared:
        l_ref[...] = jnp.zeros_like(l_ref)

    def step(j, width=1, tail=False):
        cols = width * tile
        off = pl.multiple_of(j * tile, tile)
        kc = k_ref[0, pl.ds(off, cols), :]
        vc = v_ref[0, pl.ds(off, cols), :]
        cat = lambda f: f(j) if width == 1 else jnp.concatenate([f(j + t) for t in range(width)], axis=1)
        mk = cat(lambda t: mk_ref[0, t]).astype(F32) if has_mask else None
        for a in range(2):
            ka = kc if wq == LANES else kc[:, a * LANES:(a + 1) * LANES]
            s = lax.dot_general(qs[a], ka, (((1,), (1,)), ((), ())), preferred_element_type=F32)
            if has_kbias:
                s = s + cat(lambda t: kb_ref[0, a, t])
            if tail:
                s = _finish_tail(s, (t_ref[a, 0], t_ref[a, 1]) if has_band else None, not has_mask)
            if has_mask:
                s = s + mk
            m_old = m_ref[a]
            m_new = jnp.maximum(m_old, jnp.max(s, axis=-1, keepdims=True))
            p = jnp.exp2(s - jnp.concatenate([m_new] * (cols // LANES), axis=1))
            alpha = jnp.exp2(m_old - m_new)
            va = vc if v_shared else vc[:, a * LANES:(a + 1) * LANES]
            if v_shared:
                l_ref[a] = l_ref[a] * alpha + jnp.sum(p, axis=-1, keepdims=True)
            acc_ref[a] = acc_ref[a] * alpha + jnp.dot(p.astype(BF16), va, preferred_element_type=F32)
            m_ref[a] = m_new

    first = skip_ref[pl.program_id(0), pl.program_id(3 - q_axis), i] if has_skip else 0
    n_far = jnp.maximum(i - 1, 0) - first

    def _far(t, c):
        step(first + t * FAR_WIDTH, width=FAR_WIDTH)
        return c

    lax.fori_loop(0, n_far // FAR_WIDTH, _far, 0)
    w = FAR_WIDTH // 2
    while w >= 1:
        @pl.when(n_far % (2 * w) >= w)
        def _(w=w):
            step(first + (n_far // (2 * w)) * (2 * w), width=w)
        w //= 2

    @pl.when(i >= 1)
    def _():
        step(i - 1, width=2, tail=True)

    @pl.when(i == 0)
    def _():
        step(0, tail=True)

    outs = []
    for a in range(2):
        acc = acc_ref[a]
        outs.append(acc / l_ref[a] if v_shared else acc[:, :dv] / acc[:, dv:dv + 1])
    if diff:
        d = outs[0] - lam_ref[...] * outs[1]
        ms = jnp.mean(d * d, axis=-1, keepdims=True)
        o_ref[0] = (d * lax.rsqrt(ms + RMS_EPS) * sg_ref[...]).astype(o_ref.dtype)
    else:
        o_ref[0] = jnp.concatenate(outs, axis=1).astype(o_ref.dtype)


def _flash(qa, ka, va, *, n_pairs, qoff, koff, voff, wq, dv, tile, v_shared=False, kbias=None,
           band=None, mask=None, diff=None, skip=None):
    bsz, s_len, _ = qa.shape
    nk = s_len // tile
    vw = LANES if v_shared else 2 * LANES
    assert qoff % wq == 0 and koff % wq == 0 and voff % vw == 0 and (not v_shared or dv == LANES)
    qb, kb_, vb = qoff // wq, koff // wq, voff // vw
    if mask is None:
        grid, q_axis = (bsz, n_pairs, nk), 2
        ix = lambda f: (lambda b, h, i: f(b, h, i))
    else:
        grid, q_axis = (bsz, nk, n_pairs), 1
        ix = lambda f: (lambda b, i, h: f(b, h, i))
    args = [qa, ka, va]
    specs = [pl.BlockSpec((1, tile, wq), ix(lambda b, h, i: (b, i, qb + h))),
             pl.BlockSpec((1, s_len, wq), ix(lambda b, h, i: (b, 0, kb_ + h))),
             pl.BlockSpec((1, s_len, vw), ix(lambda b, h, i: (b, 0, vb + h)))]
    if kbias is not None:
        args.append(kbias)
        specs.append(pl.BlockSpec((1, 2, nk, 1, tile), ix(lambda b, h, i: (b, h, 0, 0, 0))))
    if band is not None:
        args.append(band)
        specs.append(pl.BlockSpec((2, 2, LANES, LANES), ix(lambda b, h, i: (h, 0, 0, 0))))
    if mask is not None:
        args.append(mask)
        specs.append(pl.BlockSpec((1, nk, tile, tile), ix(lambda b, h, i: (b, 0, i, 0))))
    if diff is not None:
        for t in diff:
            args.append(t.reshape(1, LANES).astype(F32))
            specs.append(pl.BlockSpec((1, LANES), ix(lambda b, h, i: (0, 0))))
    if skip is not None:
        args.append(skip)
        specs.append(pl.BlockSpec(memory_space=pltpu.SMEM))
    return pl.pallas_call(
        functools.partial(_flash_kernel, tile=tile, wq=wq, dv=dv, v_shared=v_shared, q_axis=q_axis,
                          has_kbias=kbias is not None, has_band=band is not None,
                          has_mask=mask is not None, has_skip=skip is not None, diff=diff is not None),
        grid=grid,
        in_specs=specs,
        out_specs=pl.BlockSpec((1, tile, LANES), ix(lambda b, h, i: (b, i, h))),
        out_shape=jax.ShapeDtypeStruct((bsz, s_len, n_pairs * LANES), BF16),
        scratch_shapes=[pltpu.VMEM((2, tile, LANES), F32)] * (3 if v_shared else 2),
        compiler_params=_cparams(("parallel", "parallel", "arbitrary")),
        name="flash",
    )(*args)


def _t5_bucket(dist):
    exact = T5_BUCKETS // 2
    d = np.maximum(dist, 1).astype(np.float32)
    log_b = exact + (np.log(d / np.float32(exact)) / np.float32(math.log(T5_MAX_DIST / exact))
                     * np.float32(T5_BUCKETS - exact)).astype(np.int32)
    log_b = np.minimum(log_b, T5_BUCKETS - 1)
    return np.where(dist < exact, dist, log_b).astype(np.int32)


def _band_tiles(t5_table):
    assert T5_MAX_DIST <= LANES
    r = np.arange(LANES)[:, None]
    c = np.arange(LANES)[None, :]
    bmap = np.stack([_t5_bucket(np.maximum(r - c, 0)), _t5_bucket(LANES + r - c)])
    t = t5_table.astype(F32)
    return jnp.transpose(t[bmap] - t[T5_BUCKETS - 1], (3, 0, 1, 2)) * LOG2E


def _with_ones_col(w, heads, d):
    k = w.shape[0]
    dvp = -(-(d + 1) // LANES) * LANES
    wp = jnp.pad(w.reshape(k, heads, d), ((0, 0), (0, 0), (0, dvp - d))).reshape(k, heads * dvp)
    bias = np.zeros((heads, dvp), np.float32)
    bias[:, d] = 1.0
    return wp, jnp.asarray(bias.reshape(-1))


def _hi_lo(a):
    hi = a.astype(BF16)
    lo = (a - hi.astype(F32)).astype(BF16)
    return hi, lo


def _fox_skip(q, k, kb, tile):
    b, s, hd = q.shape
    h = kb.shape[-1]
    nk = s // tile
    norm = lambda a: jnp.sqrt(jnp.sum(jnp.square(a.astype(F32).reshape(b, nk, tile, h, hd // h)), axis=-1))
    qn, kn = norm(q), norm(k)
    kbt = kb.reshape(b, nk, tile, h)
    own = jnp.min(kbt - qn * kn, axis=2)
    reach = jnp.max(qn, axis=2)[:, :, None, :] * lax.cummax(jnp.max(kn, axis=2), axis=1)[:, None, :, :] \
        + kbt[:, :, -1, :][:, None, :, :]
    dead = reach < (own - UNDERFLOW_LOG2)[:, :, None, :]
    dead = jnp.logical_and(dead[..., 0::2], dead[..., 1::2])
    n_dead = jnp.sum(dead.astype(jnp.int32), axis=2)
    n_far = jnp.maximum(jnp.arange(nk, dtype=jnp.int32) - 1, 0)
    return jnp.transpose(jnp.minimum(n_dead, n_far[None, :, None]), (0, 2, 1))


def _pad_to(a, n, axis):
    pad = [(0, 0)] * a.ndim
    pad[axis] = (0, n - a.shape[axis])
    return jnp.pad(a, pad)


def _rope_tables(s_len):
    pos = jnp.arange(s_len, dtype=F32)
    inv = ROPE_THETA ** (-jnp.arange(0, MLA_ROPE, 2, dtype=F32) / MLA_ROPE)
    ang = pos[:, None] * inv[None, :]
    ang = jnp.concatenate([ang, ang], axis=-1)
    return jnp.cos(ang), jnp.sin(ang)


def _rot_half_matrix():
    half = MLA_ROPE // 2
    r = np.zeros((MLA_ROPE, MLA_ROPE), np.float32)
    for c in range(half):
        r[c + half, c] = -1.0
        r[c, c + half] = 1.0
    return jnp.asarray(r)


def _even_mixer(x2, bsz, s_len, g, w_in, b_forget, w_out, t5_table, tile):
    m, dm = x2.shape
    nk = s_len // tile
    scale = HEAD_DIM ** -0.5 * LOG2E
    wfq, wfk, wfv, wff, wdq, wdk, wdv, wiq, wik, wiw = jnp.split(
        w_in, [int(o) for o in np.cumsum(EVEN_SPLITS)[:-1]], axis=1)
    wfv1, ones_f = _with_ones_col(wfv, FOX_HEADS, HEAD_DIM)
    wdv1, ones_d = _with_ones_col(wdv, DSA_HEADS, HEAD_DIM)
    w_main = jnp.concatenate([wfq * scale, wfk, wfv1, wdq * scale, wdk, wdv1], axis=1).astype(BF16)
    zeros = lambda n: jnp.zeros((n,), F32)
    b_main = jnp.concatenate([zeros(2 * FOX_W), ones_f, zeros(2 * DSA_W), ones_d])
    qoff_f, koff_f, voff_f = 0, FOX_W, 2 * FOX_W
    qoff_d = voff_f + wfv1.shape[1]
    koff_d, voff_d = qoff_d + DSA_W, qoff_d + 2 * DSA_W
    main = _mm(x2, w_main, g=g, bias=b_main, out_dtype=BF16).reshape(bsz, s_len, -1)
    w_aux = jnp.concatenate([_pad_to(jnp.concatenate([wff, wiw], axis=1), LANES, 1),
                             _pad_to(wik, LANES, 1), wiq], axis=1).astype(BF16)
    aux = _mm(x2, w_aux, g=g).reshape(bsz, s_len, -1)
    ff, iw = aux[..., :FOX_HEADS], aux[..., FOX_HEADS:FOX_HEADS + IDX_HEADS]
    ik = aux[..., LANES:LANES + IDX_DIM]
    iq = aux[..., 2 * LANES:2 * LANES + IDX_HEADS * IDX_DIM]

    log_f = jax.nn.log_sigmoid(ff + b_forget.astype(F32))
    log_cum = jnp.cumsum(log_f, axis=1)
    kbias = (-LOG2E * jnp.transpose(log_cum, (0, 2, 1))).reshape(bsz, FOX_HEADS, nk, 1, tile)
    fox = _flash(main, main, main, n_pairs=FOX_HEADS // 2, qoff=qoff_f, koff=koff_f, voff=voff_f,
                 wq=LANES, dv=HEAD_DIM, tile=tile, kbias=kbias,
                 skip=_fox_skip(main[..., qoff_f:qoff_f + FOX_W], main[..., koff_f:koff_f + FOX_W],
                                -LOG2E * log_cum, tile))

    mask = _dsa_mask(iq.reshape(bsz, s_len, IDX_HEADS, IDX_DIM), ik, iw, tile, min(DSA_TOPK, s_len // 4))
    dsa = _flash(main, main, main, n_pairs=DSA_HEADS // 2, qoff=qoff_d, koff=koff_d, voff=voff_d,
                 wq=LANES, dv=HEAD_DIM, tile=tile, band=_band_tiles(t5_table), mask=mask)

    mixed = jnp.concatenate([fox, dsa], axis=-1).reshape(m, -1)
    return _mm(mixed, w_out.astype(BF16), res=x2)


def _odd_mixer(x2, bsz, s_len, g, w_in, lq1, lk1, lq2, lk2, subln_g, q_norm_g, w_uq, kv_norm_g, w_ukv,
               w_out, t5_table, lambda_init, tile):
    m, dm = x2.shape
    wcq, wck, wcv, wmq, wmkv, wmkr = jnp.split(
        w_in, [int(o) for o in np.cumsum(ODD_SPLITS)[:-1]], axis=1)
    w_main = jnp.concatenate([wcq * (HEAD_DIM ** -0.5 * LOG2E), wck, wcv], axis=1).astype(BF16)
    main = _mm(x2, w_main, g=g, out_dtype=BF16).reshape(bsz, s_len, -1)
    lat_w = MLA_Q_RANK + 2 * LANES
    w_aux = _pad_to(jnp.concatenate([wmq, wmkv, wmkr], axis=1), lat_w, 1).astype(BF16)
    aux = _mm(x2, w_aux, g=g)

    lam = (jnp.exp(jnp.sum(lq1.astype(F32) * lk1.astype(F32)))
           - jnp.exp(jnp.sum(lq2.astype(F32) * lk2.astype(F32))) + lambda_init)
    diff = _flash(main, main, main, n_pairs=DIFF_HEADS, qoff=0, koff=DIFF_QK_W, voff=2 * DIFF_QK_W,
                  wq=LANES, dv=DIFF_VDIM, tile=tile, v_shared=True, band=_band_tiles(t5_table),
                  diff=(jnp.full((LANES,), lam, F32), subln_g.astype(F32) * (1.0 - lambda_init)))

    dqk = MLA_NOPE + MLA_ROPE
    cos, sin = _rope_tables(s_len)
    rmat = _rot_half_matrix()
    head_tab = lambda first, rope: jnp.concatenate(
        [jnp.full((s_len, MLA_NOPE), first, F32), rope, jnp.zeros((s_len, LANES - dqk), F32)], axis=1)
    c1, c2 = head_tab(1.0, cos), head_tab(0.0, sin)
    wq3 = w_uq.reshape(MLA_Q_RANK, MLA_HEADS, dqk)
    wq1 = _pad_to(wq3, LANES, 2).reshape(MLA_Q_RANK, -1)
    wq2 = jnp.pad(jnp.einsum('khr,rs->khs', wq3[..., MLA_NOPE:], rmat),
                  ((0, 0), (0, 0), (MLA_NOPE, LANES - dqk))).reshape(MLA_Q_RANK, -1)
    qscale = dqk ** -0.5 * LOG2E
    mla_q = _mm(aux, wq1.astype(BF16), xblk=0, kb=MLA_Q_RANK, g=q_norm_g, w2=wq2.astype(BF16),
                rot=(c1 * qscale, c2 * qscale), out_dtype=BF16).reshape(bsz, s_len, -1)
    wkv3 = w_ukv.reshape(MLA_KV_RANK, MLA_HEADS, MLA_NOPE + MLA_VDIM)
    wk_nope = _pad_to(wkv3[..., :MLA_NOPE], LANES, 2).reshape(MLA_KV_RANK, -1)
    wv1, ones_v = _with_ones_col(wkv3[..., MLA_NOPE:].reshape(MLA_KV_RANK, -1), MLA_HEADS, MLA_VDIM)
    eye_blk = lambda mat: jnp.tile(jnp.pad(mat, ((0, 0), (MLA_NOPE, LANES - dqk))), (1, MLA_HEADS))
    kw = MLA_HEADS * LANES
    top = jnp.concatenate([wk_nope, wv1], axis=1)
    mid1 = jnp.concatenate([eye_blk(jnp.eye(MLA_ROPE, dtype=F32)), jnp.zeros((MLA_ROPE, kw), F32)], axis=1)
    mid2 = jnp.concatenate([eye_blk(rmat), jnp.zeros((MLA_ROPE, kw), F32)], axis=1)
    wk1 = _pad_to(jnp.concatenate([top, mid1], axis=0), 2 * LANES, 0)
    wk2 = _pad_to(jnp.concatenate([jnp.zeros_like(top), mid2], axis=0), 2 * LANES, 0)
    ones_tab = jnp.ones((s_len, LANES), F32)
    g_kv = jnp.concatenate([kv_norm_g.astype(F32), jnp.ones((2 * LANES - MLA_KV_RANK,), F32)])
    mla_kv = _mm(aux, wk1.astype(BF16), xblk=MLA_Q_RANK // (2 * LANES), kb=2 * LANES, g=g_kv,
                 n_norm=MLA_KV_RANK, w2=wk2.astype(BF16),
                 rot=(jnp.concatenate([c1, ones_tab], axis=1), jnp.concatenate([c2, 0.0 * ones_tab], axis=1)),
                 bias=jnp.concatenate([jnp.zeros((kw,), F32), ones_v]),
                 out_dtype=BF16).reshape(bsz, s_len, -1)
    mla = _flash(mla_q, mla_kv, mla_kv, n_pairs=MLA_HEADS // 2, qoff=0, koff=0, voff=kw,
                 wq=2 * LANES, dv=MLA_VDIM, tile=tile)

    mixed = jnp.concatenate([diff, mla], axis=-1).reshape(m, -1)
    return _mm(mixed, w_out.astype(BF16), res=x2)


def kernel(x, norm_mix_g, norm_ffn_g, w_in_even, b_forget, w_out_even, w_in_odd, lambda_q1, lambda_k1, lambda_q2, lambda_k2, diff_subln_g, mla_q_norm_g, w_mla_uq, mla_kv_norm_g, w_mla_ukv, w_out_odd, t5_bias, w_ffn_gate, w_ffn_up, w_ffn_down, final_norm_g):
    bsz, s_len, dm = x.shape
    depth = norm_mix_g.shape[0]
    tile = _pick(s_len, (512, 256, 128))
    x2 = x.astype(F32).reshape(bsz * s_len, dm)
    for layer in range(depth):
        j = layer // 2
        if layer % 2 == 0:
            x2 = _even_mixer(x2, bsz, s_len, norm_mix_g[layer], w_in_even[j], b_forget[j], w_out_even[j],
                             t5_bias, tile)
        else:
            lambda_init = 0.8 - 0.6 * math.exp(-0.3 * layer)
            x2 = _odd_mixer(x2, bsz, s_len, norm_mix_g[layer], w_in_odd[j], lambda_q1[j], lambda_k1[j],
                            lambda_q2[j], lambda_k2[j], diff_subln_g[j], mla_q_norm_g[j], w_mla_uq[j],
                            mla_kv_norm_g[j], w_mla_ukv[j], w_out_odd[j], t5_bias, lambda_init, tile)
        last = layer == depth - 1
        x2 = _ffn(x2, norm_ffn_g[layer], w_ffn_gate[layer].astype(BF16), w_ffn_up[layer].astype(BF16),
                  w_ffn_down[layer].astype(BF16), final_g=final_norm_g if last else None)
    return x2.reshape(bsz, s_len, dm)
```
